```python
import jax, jax.numpy as jnp
from jax import lax
import numpy as np

D_MODEL = 4096
BATCH = 8
SEQ = 2048
DEPTH = 2

CHUNK = 64
HEAD_DIM = 128
A_HEADS = (3 * D_MODEL // 8) // HEAD_DIM
A_WIDTH = A_HEADS * HEAD_DIM
B_HEADS = (3 * D_MODEL // 8) // HEAD_DIM
B_WIDTH = B_HEADS * HEAD_DIM
C_WIDTH = D_MODEL - A_WIDTH - B_WIDTH
POOL_WINDOWS = (2, 4, 8, 16)
POOL_GROUP = C_WIDTH // len(POOL_WINDOWS)
IN_COLS = 4 * A_WIDTH + 3 * B_WIDTH + C_WIDTH
BAND_CHUNKS = 9
MAX_REL_DIST = 128
D_FF = 256 * ((8 * D_MODEL // 3 + 255) // 256)
N_EXPERTS = 8
TOP_K = 2
D_EXPERT = 5 * D_MODEL // 4
N_DENSE = (DEPTH + 1) // 2
N_MOE = DEPTH // 2
N_MOD = 6
EPS = 1e-6
F_MIN = 1e-6
MASK_VALUE = -1e9

kernel_name = "hybrid_hgrn2_chunkattn_pool_moe_trunk"


def rms_norm(x, w):
    xf = x.astype(jnp.float32)
    y = xf * lax.rsqrt(jnp.mean(xf * xf, axis=-1, keepdims=True) + EPS)
    return (y * w.astype(jnp.float32)).astype(x.dtype)


def hgrn2_mixer(q, fz, inp, g, lb, norm_w):
    bsz, slen = q.shape[:2]
    nc = slen // CHUNK
    f32 = jnp.float32
    zf = fz.astype(f32)
    lbf = lb.astype(f32)
    f_gate = lbf + (1.0 - lbf) * jax.nn.sigmoid(zf)
    log_f = jnp.log(jnp.maximum(f_gate, F_MIN))
    key = (1.0 - lbf) * jax.nn.sigmoid(-zf)

    def heads(t):
        return t.reshape(bsz, nc, CHUNK, A_HEADS, HEAD_DIM).transpose(1, 0, 3, 2, 4)

    qh = heads(q.astype(f32))
    kh = heads(key)
    vh = heads(inp.astype(f32))
    bh = jnp.cumsum(heads(log_f), axis=3)
    causal = jnp.tril(jnp.ones((CHUNK, CHUNK), dtype=bool))

    def step(state, xs):
        qc, kc, vc, bc = xs
        inter = jnp.einsum('bhtk,bhkv->bhtv', qc * jnp.exp(bc), state)
        diff = bc[:, :, :, None, :] - bc[:, :, None, :, :]
        decay = jnp.where(causal[:, :, None], jnp.exp(jnp.minimum(diff, 0.0)), 0.0)
        scores = jnp.einsum('bhtk,bhsk,bhtsk->bhts', qc, kc, decay)
        out = inter + jnp.einsum('bhts,bhsv->bhtv', scores, vc)
        b_last = bc[:, :, -1:, :]
        state = (jnp.exp(b_last[:, :, 0, :])[..., None] * state
                 + jnp.einsum('bhsk,bhsv->bhkv', kc * jnp.exp(b_last - bc), vc))
        return state, out

    s0 = jnp.zeros((bsz, A_HEADS, HEAD_DIM, HEAD_DIM), f32)
    _, o = lax.scan(step, s0, (qh, kh, vh, bh))
    o = o.transpose(1, 0, 3, 2, 4).reshape(bsz, slen, A_HEADS, HEAD_DIM)
    gh = g.astype(f32).reshape(bsz, slen, A_HEADS, HEAD_DIM)
    o = rms_norm(o, norm_w) * jax.nn.silu(gh)
    return o.reshape(bsz, slen, A_WIDTH).astype(q.dtype)


def chunk_band_attention(q, k, v, qn_w, kn_w, rel_table):
    bsz, slen = q.shape[:2]
    nc = slen // CHUNK
    qh = rms_norm(q.reshape(bsz, slen, B_HEADS, HEAD_DIM), qn_w).transpose(0, 2, 1, 3)
    kh = rms_norm(k.reshape(bsz, slen, B_HEADS, HEAD_DIM), kn_w).transpose(0, 2, 1, 3)
    vh = v.reshape(bsz, slen, B_HEADS, HEAD_DIM).transpose(0, 2, 1, 3)
    pad = (BAND_CHUNKS - 1) * CHUNK
    band = BAND_CHUNKS * CHUNK
    k_pad = jnp.pad(kh, ((0, 0), (0, 0), (pad, 0), (0, 0)))
    v_pad = jnp.pad(vh, ((0, 0), (0, 0), (pad, 0), (0, 0)))
    rel = pad + jnp.arange(CHUNK)[:, None] - jnp.arange(band)[None, :]
    bias = rel_table.astype(jnp.float32)[:, jnp.clip(rel, -MAX_REL_DIST, MAX_REL_DIST) + MAX_REL_DIST]
    key_offset = jnp.arange(band) - pad
    scale = HEAD_DIM ** -0.5

    def one_chunk(ci):
        start = ci * CHUNK
        qc = lax.dynamic_slice_in_dim(qh, start, CHUNK, axis=2)
        kc = lax.dynamic_slice_in_dim(k_pad, start, band, axis=2)
        vc = lax.dynamic_slice_in_dim(v_pad, start, band, axis=2)
        s = jnp.einsum('bhqd,bhkd->bhqk', qc, kc).astype(jnp.float32) * scale + bias
        s = jnp.where((start + key_offset) >= 0, s, MASK_VALUE)
        p = jax.nn.softmax(s, axis=-1).astype(vc.dtype)
        return jnp.einsum('bhqk,bhkd->bhqd', p, vc)

    o = lax.map(one_chunk, jnp.arange(nc))
    return o.transpose(1, 0, 3, 2, 4).reshape(bsz, slen, B_WIDTH)


def pool_mixer(p, w_pool, pool_scale):
    slen = p.shape[1]
    pf = p.astype(jnp.float32)
    cs = jnp.pad(jnp.cumsum(pf, axis=1), ((0, 0), (1, 0), (0, 0)))
    pos = jnp.arange(slen) + 1
    outs = []
    for gi, w in enumerate(POOL_WINDOWS):
        sl = slice(gi * POOL_GROUP, (gi + 1) * POOL_GROUP)
        csg = cs[:, :, sl]
        lower = jnp.pad(csg[:, :slen + 1 - w], ((0, 0), (w - 1, 0), (0, 0)))
        cnt = jnp.minimum(pos, w).astype(jnp.float32)[None, :, None]
        mixed = (csg[:, 1:] - lower) / cnt - pf[:, :, sl]
        outs.append(jnp.einsum('bsc,cd->bsd', mixed.astype(p.dtype), w_pool[gi]))
    return jnp.concatenate(outs, axis=-1) * pool_scale


def swiglu(h, w_gate, w_up, w_down):
    return jnp.einsum('bsf,fd->bsd', jax.nn.silu(jnp.einsum('bsd,df->bsf', h, w_gate)) * jnp.einsum('bsd,df->bsf', h, w_up), w_down)


def moe_ffn(h, w_router, b_router, w_gate, w_up, w_down):
    bsz, slen, d = h.shape
    hf = h.reshape(bsz * slen, d)
    logits = jnp.einsum('td,de->te', hf, w_router).astype(jnp.float32) + b_router.astype(jnp.float32)
    top_v, top_i = lax.top_k(logits, TOP_K)
    top_w = jax.nn.softmax(top_v, axis=-1)
    combine = jnp.sum(jax.nn.one_hot(top_i, N_EXPERTS, dtype=jnp.float32) * top_w[..., None], axis=1)
    combine = combine.astype(h.dtype)
    out = jnp.zeros_like(hf)
    for e in range(N_EXPERTS):
        he = jax.nn.silu(hf @ w_gate[e]) * (hf @ w_up[e])
        out = out + combine[:, e:e + 1] * (he @ w_down[e])
    return out.reshape(bsz, slen, d)


def setup_inputs(seed: int = 0) -> dict:
    key = jax.random.key(seed)
    ks = jax.random.split(key, 24)
    f32 = jnp.float32
    nrm = lambda k, shape, s: jax.random.normal(k, shape, f32) * s
    D = D_MODEL
    return {
        "x": nrm(ks[0], (BATCH, SEQ, D), 1.0),
        "c": nrm(ks[1], (BATCH, D), 1.0),
        "w_ada": nrm(ks[2], (D, N_MOD * D), 0.5 * D ** -0.5),
        "b_ada": nrm(ks[3], (N_MOD * D,), 0.02),
        "ada_table": nrm(ks[4], (DEPTH, N_MOD, D), 0.1),
        "norm_mix_w": 1.0 + nrm(ks[5], (DEPTH, D), 0.05),
        "w_in": nrm(ks[6], (DEPTH, D, IN_COLS), D ** -0.5),
        "lb_logits": nrm(ks[7], (DEPTH, A_WIDTH), 0.1),
        "hgrn_norm_w": 1.0 + nrm(ks[8], (DEPTH, HEAD_DIM), 0.05),
        "q_norm_w": 1.0 + nrm(ks[9], (DEPTH, HEAD_DIM), 0.05),
        "k_norm_w": 1.0 + nrm(ks[10], (DEPTH, HEAD_DIM), 0.05),
        "rel_bias": nrm(ks[11], (B_HEADS, 2 * MAX_REL_DIST + 1), 0.5),
        "w_pool": nrm(ks[12], (DEPTH, len(POOL_WINDOWS), POOL_GROUP, POOL_GROUP), POOL_GROUP ** -0.5),
        "pool_scale": 1.0 + nrm(ks[13], (DEPTH, C_WIDTH), 0.1),
        "w_o": nrm(ks[14], (DEPTH, D, D), D ** -0.5),
        "norm_ffn_w": 1.0 + nrm(ks[15], (DEPTH, D), 0.05),
        "ffn_w_gate": nrm(ks[16], (N_DENSE, D, D_FF), D ** -0.5),
        "ffn_w_up": nrm(ks[17], (N_DENSE, D, D_FF), D ** -0.5),
        "ffn_w_down": nrm(ks[18], (N_DENSE, D_FF, D), D_FF ** -0.5),
        "moe_w_router": nrm(ks[19], (N_MOE, D, N_EXPERTS), D ** -0.5),
        "moe_b_router": nrm(ks[20], (N_MOE, N_EXPERTS), 0.01),
        "moe_w_gate": nrm(ks[21], (N_MOE, N_EXPERTS, D, D_EXPERT), D ** -0.5),
        "moe_w_up": nrm(ks[22], (N_MOE, N_EXPERTS, D, D_EXPERT), D ** -0.5),
        "moe_w_down": nrm(ks[23], (N_MOE, N_EXPERTS, D_EXPERT, D), D_EXPERT ** -0.5),
    }


def reference(x, c, w_ada, b_ada, ada_table, norm_mix_w, w_in, lb_logits, hgrn_norm_w,
              q_norm_w, k_norm_w, rel_bias, w_pool, pool_scale, w_o, norm_ffn_w,
              ffn_w_gate, ffn_w_up, ffn_w_down, moe_w_router, moe_b_router,
              moe_w_gate, moe_w_up, moe_w_down):
    bsz = x.shape[0]
    mod = (jnp.einsum('bd,de->be', jax.nn.silu(c), w_ada) + b_ada).reshape(bsz, N_MOD, D_MODEL)
    lb_p = jax.nn.softmax(lb_logits.astype(jnp.float32), axis=0)
    lb_all = jnp.cumsum(lb_p, axis=0) - lb_p[0:1]
    a0, a1, a2, a3 = A_WIDTH, 2 * A_WIDTH, 3 * A_WIDTH, 4 * A_WIDTH
    b1, b2, b3 = a3 + B_WIDTH, a3 + 2 * B_WIDTH, a3 + 3 * B_WIDTH
    for l in range(DEPTH):
        m = (mod + ada_table[l][None]).astype(x.dtype)
        shift1, scale1, gate1 = m[:, 0, None], m[:, 1, None], m[:, 2, None]
        shift2, scale2, gate2 = m[:, 3, None], m[:, 4, None], m[:, 5, None]
        h = rms_norm(x, norm_mix_w[l]) * (1.0 + scale1) + shift1
        proj = jnp.einsum('bsd,de->bse', h, w_in[l])
        y_a = hgrn2_mixer(proj[..., :a0], proj[..., a0:a1], proj[..., a1:a2], proj[..., a2:a3],
                          lb_all[l], hgrn_norm_w[l])
        y_b = chunk_band_attention(proj[..., a3:b1], proj[..., b1:b2], proj[..., b2:b3],
                                   q_norm_w[l], k_norm_w[l], rel_bias)
        y_c = pool_mixer(proj[..., b3:], w_pool[l], pool_scale[l])
        y = jnp.einsum('bse,ed->bsd', jnp.concatenate([y_a, y_b, y_c], axis=-1), w_o[l])
        x = x + gate1 * y
        h = rms_norm(x, norm_ffn_w[l]) * (1.0 + scale2) + shift2
        if l % 2 == 0:
            f = swiglu(h, ffn_w_gate[l // 2], ffn_w_up[l // 2], ffn_w_down[l // 2])
        else:
            j = l // 2
            f = moe_ffn(h, moe_w_router[j], moe_b_router[j], moe_w_gate[j], moe_w_up[j], moe_w_down[j])
        x = x + gate2 * f
    return x
```

```python
import functools

import jax
import jax.numpy as jnp
from jax import lax
from jax.experimental import pallas as pl
from jax.experimental.pallas import tpu as pltpu

F32 = jnp.float32
BF16 = jnp.bfloat16
U32 = jnp.uint32
I32 = jnp.int32

D_MODEL = 4096
CHUNK = 64
HEAD_DIM = 128
A_HEADS = 12
B_HEADS = 12
A_WIDTH = A_HEADS * HEAD_DIM
B_WIDTH = B_HEADS * HEAD_DIM
C_WIDTH = D_MODEL - A_WIDTH - B_WIDTH
POOL_WINDOWS = (2, 4, 8, 16)
POOL_GROUP = C_WIDTH // len(POOL_WINDOWS)
IN_COLS = 4 * A_WIDTH + 3 * B_WIDTH + C_WIDTH
IN_HEADS = IN_COLS // HEAD_DIM
BAND_CHUNKS = 9
BAND = BAND_CHUNKS * CHUNK
BAND_PAD = (BAND_CHUNKS - 1) * CHUNK
MAX_REL_DIST = 128
N_EXPERTS = 8
N_MOD = 6
EPS = 1e-6
F_MIN = 1e-6
MASK_VALUE = -1e9

LANES = 128
VMEM_LIMIT_MB = 56

TM = 1024
TN = 512
TN_GLU = 256
TM_E = 512
TM_NORM = 256
TM_ROUTE = 128
TM_COMB = 128
CH_DISP = 128


def _cparams(n_grid, vmem_mb=VMEM_LIMIT_MB):
    return pltpu.CompilerParams(
        dimension_semantics=("arbitrary",) * n_grid,
        vmem_limit_bytes=vmem_mb * 1024 * 1024,
    )


def _silu(x):
    return x * jax.nn.sigmoid(x)


def _cast_weight(w_ref, wbf_ref):
    k = w_ref.shape[0]
    rows = 512 if k % 512 == 0 else 128
    assert k % rows == 0

    def body(r, carry):
        sl = pl.ds(pl.multiple_of(r * rows, rows), rows)
        wbf_ref[sl, :] = w_ref[sl, :].astype(BF16)
        return carry

    lax.fori_loop(0, k // rows, body, 0)


def _mod_kernel(c_ref, w_ref, b_ref, o_ref):
    s = _silu(c_ref[...]).astype(BF16)
    w = w_ref[...].astype(BF16)
    o_ref[...] = jnp.dot(s, w, preferred_element_type=F32) + b_ref[...]


def _adaln_mod(c, w_ada, b_ada):
    bsz, d = c.shape
    n = w_ada.shape[1]
    tn = 1024
    return pl.pallas_call(
        _mod_kernel,
        grid=(n // tn,),
        in_specs=[
            pl.BlockSpec((bsz, d), lambda j: (0, 0)),
            pl.BlockSpec((d, tn), lambda j: (0, j)),
            pl.BlockSpec((1, tn), lambda j: (0, j)),
        ],
        out_specs=pl.BlockSpec((bsz, tn), lambda j: (0, j)),
        out_shape=jax.ShapeDtypeStruct((bsz, n), F32),
        compiler_params=_cparams(1),
        name="adaln_mod",
    )(c, w_ada, b_ada.reshape(1, n))


def _modulated_norm(x_ref, mod_ref, ada_ref, w_ref, shift_idx):
    x = x_ref[...]
    y = x * lax.rsqrt(jnp.mean(x * x, axis=-1, keepdims=True) + EPS) * w_ref[...]
    m = mod_ref[0] + ada_ref[...]
    shift = m[shift_idx:shift_idx + 1]
    scale = m[shift_idx + 1:shift_idx + 2]
    return y * (1.0 + scale) + shift


def _norm_kernel(x_ref, mod_ref, ada_ref, w_ref, h_ref, *, shift_idx):
    h_ref[...] = _modulated_norm(x_ref, mod_ref, ada_ref, w_ref, shift_idx).astype(BF16)


def _pack_bf16_pairs(h):
    n = h.shape[1] // 2
    lo = lax.bitcast_convert_type(h[:, :n].astype(BF16).astype(F32), U32)
    hi = lax.bitcast_convert_type(h[:, n:].astype(BF16).astype(F32), U32)
    return (hi & jnp.uint32(0xFFFF0000)) | (lo >> 16)


def _unpack_bf16_pairs(u):
    lo = lax.bitcast_convert_type(u << 16, F32).astype(BF16)
    hi = lax.bitcast_convert_type(u & jnp.uint32(0xFFFF0000), F32).astype(BF16)
    return lo, hi


def _norm_router_kernel(x_ref, mod_ref, ada_ref, w_ref, wr_ref, br_ref, hp_ref, lg_ref, *, shift_idx):
    h = _modulated_norm(x_ref, mod_ref, ada_ref, w_ref, shift_idx)
    hp_ref[...] = _pack_bf16_pairs(h)
    wr = wr_ref[...]
    h_hi = h.astype(BF16)
    h_lo = (h - h_hi.astype(F32)).astype(BF16)
    w_hi = wr.astype(BF16)
    w_lo = (wr - w_hi.astype(F32)).astype(BF16)
    lg = jnp.dot(h_hi, w_hi, preferred_element_type=F32)
    lg = lg + jnp.dot(h_hi, w_lo, preferred_element_type=F32)
    lg = lg + jnp.dot(h_lo, w_hi, preferred_element_type=F32)
    lg_ref[...] = lg + br_ref[...]


def _norm_specs(t, d, s_len, tm, layer):
    per_b = s_len // tm
    return [
        pl.BlockSpec((tm, d), lambda i: (i, 0)),
        pl.BlockSpec((1, N_MOD, d), lambda i: (i // per_b, 0, 0)),
        pl.BlockSpec((None, N_MOD, d), lambda i: (layer, 0, 0)),
        pl.BlockSpec((None, 1, d), lambda i: (layer, 0, 0)),
    ]


def _norm(x2, mod, ada_table, norm_w, layer, s_len, shift_idx):
    t, d = x2.shape
    tm = min(TM_NORM, s_len)
    return pl.pallas_call(
        functools.partial(_norm_kernel, shift_idx=shift_idx),
        grid=(t // tm,),
        in_specs=_norm_specs(t, d, s_len, tm, layer),
        out_specs=pl.BlockSpec((tm, d), lambda i: (i, 0)),
        out_shape=jax.ShapeDtypeStruct((t, d), BF16),
        compiler_params=_cparams(1),
        name="norm_mod",
    )(x2, mod, ada_table, norm_w.reshape(norm_w.shape[0], 1, d))


def _norm_router(x2, mod, ada_table, norm_w, w_router_pad, b_router_pad, layer, moe_idx, s_len, shift_idx):
    t, d = x2.shape
    tm = min(TM_NORM, s_len)
    return pl.pallas_call(
        functools.partial(_norm_router_kernel, shift_idx=shift_idx),
        grid=(t // tm,),
        in_specs=_norm_specs(t, d, s_len, tm, layer) + [
            pl.BlockSpec((None, d, LANES), lambda i: (moe_idx, 0, 0)),
            pl.BlockSpec((None, 1, LANES), lambda i: (moe_idx, 0, 0)),
        ],
        out_specs=[
            pl.BlockSpec((tm, d // 2), lambda i: (i, 0)),
            pl.BlockSpec((tm, LANES), lambda i: (i, 0)),
        ],
        out_shape=[
            jax.ShapeDtypeStruct((t, d // 2), U32),
            jax.ShapeDtypeStruct((t, LANES), F32),
        ],
        compiler_params=_cparams(1),
        name="norm_mod_router",
    )(x2, mod, ada_table, norm_w.reshape(norm_w.shape[0], 1, d), w_router_pad, b_router_pad)


def _inproj_kernel(h_ref, w_ref, o_ref, wbf_ref):
    @pl.when(pl.program_id(1) == 0)
    def _():
        _cast_weight(w_ref, wbf_ref)

    acc = jnp.dot(h_ref[...], wbf_ref[...], preferred_element_type=F32)
    for hh in range(o_ref.shape[1]):
        o_ref[0, hh] = acc[:, hh * HEAD_DIM:(hh + 1) * HEAD_DIM].astype(BF16)


def _in_proj(h, w_in, layer, bsz, s_len):
    t, d = h.shape
    n = w_in.shape[2]
    tm = min(TM, s_len)
    tn = TN
    per_b = s_len // tm
    hpt = tn // HEAD_DIM
    return pl.pallas_call(
        _inproj_kernel,
        grid=(n // tn, t // tm),
        in_specs=[
            pl.BlockSpec((tm, d), lambda j, i: (i, 0)),
            pl.BlockSpec((None, d, tn), lambda j, i: (layer, 0, j)),
        ],
        out_specs=pl.BlockSpec((1, hpt, tm, HEAD_DIM), lambda j, i: (i // per_b, j, i % per_b, 0)),
        out_shape=jax.ShapeDtypeStruct((bsz, n // HEAD_DIM, s_len, HEAD_DIM), BF16),
        scratch_shapes=[pltpu.VMEM((d, tn), BF16)],
        compiler_params=_cparams(2),
        name="in_proj",
    )(h, w_in)


_HGRN_LEVELS = (32, 16, 8, 4)
_HGRN_DIRECT = 4


def _row_bcast(x, period, row):
    n, c = x.shape
    x3 = x.reshape(n // period, period, c)
    return jnp.broadcast_to(x3[:, row:row + 1, :], x3.shape).reshape(n, c)


def _hgrn_kernel(q_ref, z_ref, v_ref, g_ref, lb_ref, nw_ref, o_ref, st_ref):
    s_len = q_ref.shape[2]
    nc = s_len // CHUNK
    lb = lb_ref[0]
    nw = nw_ref[...]
    row = lax.broadcasted_iota(I32, (CHUNK, HEAD_DIM), 0)
    r64 = lax.broadcasted_iota(I32, (CHUNK, CHUNK), 0)
    c64 = lax.broadcasted_iota(I32, (CHUNK, CHUNK), 1)
    level_masks = []
    for hs in _HGRN_LEVELS:
        pair = 2 * hs
        same = (r64 & ~(pair - 1)) == (c64 & ~(pair - 1))
        level_masks.append(same & ((r64 & (pair - 1)) >= hs) & ((c64 & (pair - 1)) < hs))

    st_ref[...] = jnp.zeros_like(st_ref)

    def chunk(ci, carry):
        sl = pl.ds(pl.multiple_of(ci * CHUNK, CHUNK), CHUNK)
        q = q_ref[0, 0, sl, :].astype(F32)
        z = z_ref[0, 0, sl, :].astype(F32)
        v_bf = v_ref[0, 0, sl, :]
        v = v_bf.astype(F32)
        g = g_ref[0, 0, sl, :].astype(F32)

        e = jnp.exp(-jnp.abs(z))
        r = 1.0 / (1.0 + e)
        sig_pos = jnp.where(z >= 0, r, e * r)
        sig_neg = jnp.where(z >= 0, e * r, r)
        f_gate = lb + (1.0 - lb) * sig_pos
        log_f = jnp.log(jnp.maximum(f_gate, F_MIN))
        key = (1.0 - lb) * sig_neg

        b = log_f
        for sh in (1, 2, 4, 8, 16, 32):
            b = b + jnp.where(row >= sh, pltpu.roll(b, sh, axis=0), 0.0)
        b_last = b[CHUNK - 1:CHUNK, :]

        scores = jnp.zeros((CHUNK, CHUNK), F32)
        for hs, mask in zip(_HGRN_LEVELS, level_masks):
            ref = _row_bcast(b, 2 * hs, hs - 1)
            upper = (row & (2 * hs - 1)) >= hs
            w = jnp.exp(-jnp.abs(b - ref))
            a_l = jnp.where(upper, q * w, 0.0).astype(BF16)
            k_l = jnp.where(upper, 0.0, key * w).astype(BF16)
            s_l = lax.dot_general(a_l, k_l, (((1,), (1,)), ((), ())), preferred_element_type=F32)
            scores = scores + jnp.where(mask, s_l, 0.0)
        intra = jnp.dot(scores.astype(BF16), v_bf, preferred_element_type=F32)

        direct = jnp.sum(q * key, axis=-1, keepdims=True) * v
        for d in range(1, _HGRN_DIRECT):
            ks = pltpu.roll(key, d, axis=0)
            bs = pltpu.roll(b, d, axis=0)
            vs = pltpu.roll(v, d, axis=0)
            wgt = jnp.sum(q * ks * jnp.exp(jnp.minimum(b - bs, 0.0)), axis=-1, keepdims=True)
            direct = direct + jnp.where((row & (_HGRN_DIRECT - 1)) >= d, wgt, 0.0) * vs

        st = st_ref[...]
        qe = (q * jnp.exp(b)).astype(BF16)
        inter = lax.dot_general(qe, st.astype(BF16), (((1,), (1,)), ((), ())), preferred_element_type=F32)
        kd = (key * jnp.exp(b_last - b)).astype(BF16)
        upd = lax.dot_general(v_bf, kd, (((0,), (0,)), ((), ())), preferred_element_type=F32)
        st_ref[...] = st * jnp.exp(b_last) + upd

        out = direct + intra + inter
        y = out * lax.rsqrt(jnp.mean(out * out, axis=-1, keepdims=True) + EPS) * nw
        o_ref[0, sl, :] = (y * _silu(g)).astype(BF16)
        return carry

    lax.fori_loop(0, nc, chunk, 0)


def _hgrn(proj_hm, lb_l, norm_w_l):
    bsz, _, s_len, _ = proj_hm.shape

    def slab(off):
        return pl.BlockSpec((1, 1, s_len, HEAD_DIM), lambda b, h: (b, off + h, 0, 0))

    return pl.pallas_call(
        _hgrn_kernel,
        grid=(bsz, A_HEADS),
        in_specs=[
            slab(0), slab(A_HEADS), slab(2 * A_HEADS), slab(3 * A_HEADS),
            pl.BlockSpec((1, 1, HEAD_DIM), lambda b, h: (h, 0, 0)),
            pl.BlockSpec((1, HEAD_DIM), lambda b, h: (0, 0)),
        ],
        out_specs=pl.BlockSpec((1, s_len, HEAD_DIM), lambda b, h: (b, 0, h)),
        out_shape=jax.ShapeDtypeStruct((bsz, s_len, A_WIDTH), BF16),
        scratch_shapes=[pltpu.VMEM((HEAD_DIM, HEAD_DIM), F32)],
        compiler_params=_cparams(2),
        name="hgrn2",
    )(proj_hm, proj_hm, proj_hm, proj_hm, lb_l, norm_w_l)


def _head_rms(x, w):
    return x * lax.rsqrt(jnp.mean(x * x, axis=-1, keepdims=True) + EPS) * w


def _attn_kernel(q_ref, k_ref, v_ref, bias_ref, qw_ref, kw_ref, o_ref, kp_ref, vp_ref):
    s_len = q_ref.shape[2]
    nc = s_len // CHUNK
    qw = qw_ref[...]
    kw = kw_ref[...]
    bias = bias_ref[0]
    scale = HEAD_DIM ** -0.5
    col = lax.broadcasted_iota(I32, (CHUNK, BAND), 1)

    kp_ref[0:BAND_PAD, :] = jnp.zeros((BAND_PAD, HEAD_DIM), BF16)
    vp_ref[0:BAND_PAD, :] = jnp.zeros((BAND_PAD, HEAD_DIM), BF16)
    blk = 256 if s_len % 256 == 0 else CHUNK

    def prep(i, carry):
        src = pl.ds(pl.multiple_of(i * blk, blk), blk)
        dst = pl.ds(pl.multiple_of(BAND_PAD + i * blk, CHUNK), blk)
        kp_ref[dst, :] = _head_rms(k_ref[0, 0, src, :].astype(F32), kw).astype(BF16)
        vp_ref[dst, :] = v_ref[0, 0, src, :]
        return carry

    lax.fori_loop(0, s_len // blk, prep, 0)

    def chunk(ci, carry):
        r0 = pl.multiple_of(ci * CHUNK, CHUNK)
        qn = _head_rms(q_ref[0, 0, pl.ds(r0, CHUNK), :].astype(F32), qw).astype(BF16)
        kc = kp_ref[pl.ds(r0, BAND), :]
        vc = vp_ref[pl.ds(r0, BAND), :]
        s = lax.dot_general(qn, kc, (((1,), (1,)), ((), ())), preferred_element_type=F32) * scale + bias
        s = jnp.where(col + r0 >= BAND_PAD, s, MASK_VALUE)
        m = jnp.max(s, axis=-1, keepdims=True)
        p = jnp.exp(s - m)
        den = jnp.sum(p, axis=-1, keepdims=True)
        o = jnp.dot(p.astype(BF16), vc, preferred_element_type=F32) / den
        o_ref[0, pl.ds(r0, CHUNK), :] = o.astype(BF16)
        return carry

    lax.fori_loop(0, nc, chunk, 0)


def _band_attention(proj_hm, bias, qn_w_l, kn_w_l):
    bsz, _, s_len, _ = proj_hm.shape
    base = 4 * A_HEADS

    def slab(off):
        return pl.BlockSpec((1, 1, s_len, HEAD_DIM), lambda b, h: (b, base + off + h, 0, 0))

    return pl.pallas_call(
        _attn_kernel,
        grid=(bsz, B_HEADS),
        in_specs=[
            slab(0), slab(B_HEADS), slab(2 * B_HEADS),
            pl.BlockSpec((1, CHUNK, BAND), lambda b, h: (h, 0, 0)),
            pl.BlockSpec((1, HEAD_DIM), lambda b, h: (0, 0)),
            pl.BlockSpec((1, HEAD_DIM), lambda b, h: (0, 0)),
        ],
        out_specs=pl.BlockSpec((1, s_len, HEAD_DIM), lambda b, h: (b, 0, h)),
        out_shape=jax.ShapeDtypeStruct((bsz, s_len, B_WIDTH), BF16),
        scratch_shapes=[
            pltpu.VMEM((BAND_PAD + s_len, HEAD_DIM), BF16),
            pltpu.VMEM((BAND_PAD + s_len, HEAD_DIM), BF16),
        ],
        compiler_params=_cparams(2),
        name="band_attention",
    )(proj_hm, proj_hm, proj_hm, bias, qn_w_l, kn_w_l)


_POOL_HALO = 16


def _pool_kernel(p_ref, w_ref, sc_ref, o_ref, pad_ref, wbf_ref):
    s_len = p_ref.shape[2]
    gi = pl.program_id(1)
    blk = 256 if s_len % 256 == 0 else CHUNK
    nblk = s_len // blk
    hp = p_ref.shape[1]
    pad_ref[0:_POOL_HALO, :] = jnp.zeros((_POOL_HALO, pad_ref.shape[1]), F32)
    for hh in range(hp):
        pad_ref[_POOL_HALO:_POOL_HALO + s_len, hh * HEAD_DIM:(hh + 1) * HEAD_DIM] = p_ref[0, hh].astype(F32)
    wbf_ref[...] = w_ref[0].astype(BF16)
    scale = sc_ref[...]

    for g, win in enumerate(POOL_WINDOWS):
        @pl.when(gi == g)
        def _(win=win):
            def body(i, carry):
                r0 = pl.multiple_of(i * blk, blk)
                acc = pad_ref[pl.ds(r0, _POOL_HALO + blk), :]
                cur = acc[_POOL_HALO:, :]
                span = 1
                while span < win:
                    acc = acc + pltpu.roll(acc, span, axis=0)
                    span *= 2
                acc = acc[_POOL_HALO:, :]
                pos = lax.broadcasted_iota(I32, acc.shape, 0) + r0 + 1
                cnt = jnp.minimum(pos, win).astype(F32)
                mixed = (acc / cnt - cur).astype(BF16)
                out = jnp.dot(mixed, wbf_ref[...], preferred_element_type=F32) * scale
                o_ref[0, pl.ds(r0, blk), :] = out.astype(BF16)
                return carry

            lax.fori_loop(0, nblk, body, 0)


def _pool_mixer(proj_hm, w_pool, pool_scale3, layer):
    bsz, _, s_len, _ = proj_hm.shape
    hp = POOL_GROUP // HEAD_DIM
    base = (4 * A_HEADS + 3 * B_HEADS) // hp
    n_grp = len(POOL_WINDOWS)
    return pl.pallas_call(
        _pool_kernel,
        grid=(bsz, n_grp),
        in_specs=[
            pl.BlockSpec((1, hp, s_len, HEAD_DIM), lambda b, g: (b, base + g, 0, 0)),
            pl.BlockSpec((None, 1, POOL_GROUP, POOL_GROUP), lambda b, g: (layer, g, 0, 0)),
            pl.BlockSpec((None, 1, POOL_GROUP), lambda b, g: (layer, 0, g)),
        ],
        out_specs=pl.BlockSpec((1, s_len, POOL_GROUP), lambda b, g: (b, 0, g)),
        out_shape=jax.ShapeDtypeStruct((bsz, s_len, C_WIDTH), BF16),
        scratch_shapes=[
            pltpu.VMEM((_POOL_HALO + s_len, POOL_GROUP), F32),
            pltpu.VMEM((POOL_GROUP, POOL_GROUP), BF16),
        ],
        compiler_params=_cparams(2),
        name="pool_mixer",
    )(proj_hm, w_pool, pool_scale3)


def _gate_row(mod_ref, ada_ref, gate_idx):
    return mod_ref[0, gate_idx:gate_idx + 1, :] + ada_ref[gate_idx:gate_idx + 1, :]


def _outproj_kernel(ya_ref, yb_ref, yc_ref, w_ref, x_ref, mod_ref, ada_ref, o_ref, wbf_ref, *, gate_idx):
    @pl.when(pl.program_id(1) == 0)
    def _():
        _cast_weight(w_ref, wbf_ref)

    acc = jnp.dot(ya_ref[...], wbf_ref[0:A_WIDTH, :], preferred_element_type=F32)
    acc = acc + jnp.dot(yb_ref[...], wbf_ref[A_WIDTH:A_WIDTH + B_WIDTH, :], preferred_element_type=F32)
    acc = acc + jnp.dot(yc_ref[...], wbf_ref[A_WIDTH + B_WIDTH:, :], preferred_element_type=F32)
    o_ref[...] = x_ref[...] + _gate_row(mod_ref, ada_ref, gate_idx) * acc


def _out_proj(ya, yb, yc, w_o, x2, mod, ada_table, layer, s_len, gate_idx):
    t, d = x2.shape
    tm = min(TM, s_len)
    tn = TN
    per_b = s_len // tm
    return pl.pallas_call(
        functools.partial(_outproj_kernel, gate_idx=gate_idx),
        grid=(d // tn, t // tm),
        in_specs=[
            pl.BlockSpec((tm, A_WIDTH), lambda j, i: (i, 0)),
            pl.BlockSpec((tm, B_WIDTH), lambda j, i: (i, 0)),
            pl.BlockSpec((tm, C_WIDTH), lambda j, i: (i, 0)),
            pl.BlockSpec((None, d, tn), lambda j, i: (layer, 0, j)),
            pl.BlockSpec((tm, tn), lambda j, i: (i, j)),
            pl.BlockSpec((1, N_MOD, tn), lambda j, i: (i // per_b, 0, j)),
            pl.BlockSpec((None, N_MOD, tn), lambda j, i: (layer, 0, j)),
        ],
        out_specs=pl.BlockSpec((tm, tn), lambda j, i: (i, j)),
        out_shape=jax.ShapeDtypeStruct((t, d), F32),
        scratch_shapes=[pltpu.VMEM((d, tn), BF16)],
        compiler_params=_cparams(2),
        name="out_proj",
    )(ya, yb, yc, w_o, x2, mod, ada_table)


def _tile_flags(te_ref, n_tiles):
    i = pl.program_id(1)
    prev = te_ref[jnp.maximum(i - 1, 0)]
    first = jnp.logical_or(i == 0, te_ref[i] != prev)
    valid = i < te_ref[n_tiles]
    return first, valid


def _glu_kernel(te_ref, x_ref, wg_ref, wu_ref, o_ref, wg_bf, wu_bf, *, n_tiles, packed):
    first, valid = _tile_flags(te_ref, n_tiles)

    @pl.when(first)
    def _():
        _cast_weight(wg_ref, wg_bf)
        _cast_weight(wu_ref, wu_bf)

    @pl.when(valid)
    def _():
        if packed:
            half = x_ref.shape[1]
            lo, hi = _unpack_bf16_pairs(x_ref[...])
            g = jnp.dot(lo, wg_bf[0:half, :], preferred_element_type=F32)
            g = g + jnp.dot(hi, wg_bf[half:, :], preferred_element_type=F32)
            u = jnp.dot(lo, wu_bf[0:half, :], preferred_element_type=F32)
            u = u + jnp.dot(hi, wu_bf[half:, :], preferred_element_type=F32)
        else:
            x = x_ref[...]
            g = jnp.dot(x, wg_bf[...], preferred_element_type=F32)
            u = jnp.dot(x, wu_bf[...], preferred_element_type=F32)
        o_ref[...] = (_silu(g) * u).astype(BF16)

    @pl.when(jnp.logical_not(valid))
    def _():
        o_ref[...] = jnp.zeros_like(o_ref)


def _glu(x, wg, wu, te, widx, tm, packed):
    rows, xc = x.shape
    d, f = wg.shape[-2], wg.shape[-1]
    tn = TN_GLU
    n_tiles = rows // tm
    lead = len(wg.shape) - 3

    def w_map(j, i, te_ref):
        return tuple(widx[:lead]) + (te_ref[i], 0, j)

    w_block = (None,) * (lead + 1) + (d, tn)
    grid_spec = pltpu.PrefetchScalarGridSpec(
        num_scalar_prefetch=1,
        grid=(f // tn, n_tiles),
        in_specs=[
            pl.BlockSpec((tm, xc), lambda j, i, te_ref: (i, 0)),
            pl.BlockSpec(w_block, w_map),
            pl.BlockSpec(w_block, w_map),
        ],
        out_specs=pl.BlockSpec((tm, tn), lambda j, i, te_ref: (i, j)),
        scratch_shapes=[pltpu.VMEM((d, tn), BF16), pltpu.VMEM((d, tn), BF16)],
    )
    return pl.pallas_call(
        functools.partial(_glu_kernel, n_tiles=n_tiles, packed=packed),
        grid_spec=grid_spec,
        out_shape=jax.ShapeDtypeStruct((rows, f), BF16),
        compiler_params=_cparams(2),
        name="glu_pair",
    )(te, x, wg, wu)


def _down_res_kernel(a_ref, w_ref, r_ref, mod_ref, ada_ref, o_ref, wbf_ref, *, gate_idx):
    @pl.when(pl.program_id(1) == 0)
    def _():
        _cast_weight(w_ref, wbf_ref)

    acc = jnp.dot(a_ref[...], wbf_ref[...], preferred_element_type=F32)
    o_ref[...] = r_ref[...] + _gate_row(mod_ref, ada_ref, gate_idx) * acc


def _down_residual(a, w_down, widx, k_half, n_half, res, mod, ada_table, layer, s_len, gate_idx):
    t, d = res.shape
    kh = a.shape[1] // n_half
    tm = min(TM_E, s_len)
    tn = TN
    per_b = s_len // tm
    return pl.pallas_call(
        functools.partial(_down_res_kernel, gate_idx=gate_idx),
        grid=(d // tn, t // tm),
        in_specs=[
            pl.BlockSpec((tm, kh), lambda j, i: (i, k_half)),
            pl.BlockSpec((None, kh, tn), lambda j, i: (widx, k_half, j)),
            pl.BlockSpec((tm, tn), lambda j, i: (i, j)),
            pl.BlockSpec((1, N_MOD, tn), lambda j, i: (i // per_b, 0, j)),
            pl.BlockSpec((None, N_MOD, tn), lambda j, i: (layer, 0, j)),
        ],
        out_specs=pl.BlockSpec((tm, tn), lambda j, i: (i, j)),
        out_shape=jax.ShapeDtypeStruct((t, d), F32),
        scratch_shapes=[pltpu.VMEM((kh, tn), BF16)],
        compiler_params=_cparams(2),
        name="down_residual",
    )(a, w_down, res, mod, ada_table)


def _down_group_kernel(te_ref, a_ref, w_ref, o_ref, wbf_ref, *, n_tiles):
    first, valid = _tile_flags(te_ref, n_tiles)

    @pl.when(first)
    def _():
        _cast_weight(w_ref, wbf_ref)

    @pl.when(valid)
    def _():
        o_ref[...] = jnp.dot(a_ref[...], wbf_ref[...], preferred_element_type=F32)

    @pl.when(jnp.logical_not(valid))
    def _():
        o_ref[...] = jnp.zeros_like(o_ref)


def _down_grouped(a, w_down, moe_idx, te, tm):
    rows, k = a.shape
    d = w_down.shape[-1]
    tn = TN
    n_tiles = rows // tm
    grid_spec = pltpu.PrefetchScalarGridSpec(
        num_scalar_prefetch=1,
        grid=(d // tn, n_tiles),
        in_specs=[
            pl.BlockSpec((tm, k), lambda j, i, te_ref: (i, 0)),
            pl.BlockSpec((None, None, k, tn), lambda j, i, te_ref: (moe_idx, te_ref[i], 0, j)),
        ],
        out_specs=pl.BlockSpec((tm, tn), lambda j, i, te_ref: (i, j)),
        scratch_shapes=[pltpu.VMEM((k, tn), BF16)],
    )
    return pl.pallas_call(
        functools.partial(_down_group_kernel, n_tiles=n_tiles),
        grid_spec=grid_spec,
        out_shape=jax.ShapeDtypeStruct((rows, d), F32),
        compiler_params=_cparams(2),
        name="down_grouped",
    )(te, a, w_down)


def _route_kernel(lg_ref, tbl_ref, cnt_ref, carry_ref):
    i = pl.program_id(0)

    @pl.when(i == 0)
    def _():
        carry_ref[...] = jnp.zeros_like(carry_ref)

    tm = lg_ref.shape[0]
    lane = lax.broadcasted_iota(I32, (tm, LANES), 1).astype(F32)
    neg = jnp.float32(-jnp.inf)
    lg = jnp.where(lane < N_EXPERTS, lg_ref[...], neg)
    m1 = jnp.max(lg, axis=-1, keepdims=True)
    e1 = jnp.min(jnp.where(lg == m1, lane, float(LANES)), axis=-1, keepdims=True)
    lg2 = jnp.where(lane == e1, neg, lg)
    m2 = jnp.max(lg2, axis=-1, keepdims=True)
    e2 = jnp.min(jnp.where(lg2 == m2, lane, float(LANES)), axis=-1, keepdims=True)
    tt = jnp.exp(m2 - m1)
    w1 = 1.0 / (1.0 + tt)
    w2 = tt * w1

    memb = jnp.logical_or(lane == e1, lane == e2).astype(F32)
    rr = lax.broadcasted_iota(I32, (tm, tm), 0)
    cc = lax.broadcasted_iota(I32, (tm, tm), 1)
    strict_lower = (cc < rr).astype(BF16)
    pos = jnp.dot(strict_lower, memb.astype(BF16), preferred_element_type=F32) + carry_ref[...]
    p1 = jnp.sum(jnp.where(lane == e1, pos, 0.0), axis=-1, keepdims=True)
    p2 = jnp.sum(jnp.where(lane == e2, pos, 0.0), axis=-1, keepdims=True)
    carry_ref[...] = carry_ref[...] + jnp.sum(memb, axis=0, keepdims=True)

    out = jnp.where(lane == 0, e1, 0.0)
    out = jnp.where(lane == 1, e2, out)
    out = jnp.where(lane == 2, p1, out)
    out = jnp.where(lane == 3, p2, out)
    out = jnp.where(lane == 4, w1, out)
    out = jnp.where(lane == 5, w2, out)
    tbl_ref[...] = out
    cnt_ref[...] = carry_ref[...]


def _route(logits):
    t = logits.shape[0]
    tm = TM_ROUTE
    return pl.pallas_call(
        _route_kernel,
        grid=(t // tm,),
        in_specs=[pl.BlockSpec((tm, LANES), lambda i: (i, 0))],
        out_specs=[
            pl.BlockSpec((tm, LANES), lambda i: (i, 0)),
            pl.BlockSpec((1, LANES), lambda i: (0, 0)),
        ],
        out_shape=[
            jax.ShapeDtypeStruct((t, LANES), F32),
            jax.ShapeDtypeStruct((1, LANES), F32),
        ],
        scratch_shapes=[pltpu.VMEM((1, LANES), F32)],
        compiler_params=_cparams(1),
        name="moe_route",
    )(logits)


def _dispatch_kernel(dest_ref, hp_ref, xs_in_ref, xs_ref, sem):
    del xs_in_ref
    i = pl.program_id(0)
    ch = dest_ref.shape[0] // 2

    def issue(r, carry):
        src = hp_ref.at[pl.ds(i * ch + r, 1)]
        pltpu.make_async_copy(src, xs_ref.at[pl.ds(dest_ref[2 * r], 1)], sem).start()
        pltpu.make_async_copy(src, xs_ref.at[pl.ds(dest_ref[2 * r + 1], 1)], sem).start()
        return carry

    lax.fori_loop(0, ch, issue, 0)

    def drain(r, carry):
        pltpu.make_async_copy(hp_ref.at[pl.ds(0, 1)], xs_ref.at[pl.ds(0, 1)], sem).wait()
        return carry

    lax.fori_loop(0, 2 * ch, drain, 0)


def _dispatch(hp, dest_flat, rows):
    t, w = hp.shape
    ch = CH_DISP
    xs0 = jnp.zeros((rows, w), hp.dtype)
    return pl.pallas_call(
        _dispatch_kernel,
        grid=(t // ch,),
        in_specs=[
            pl.BlockSpec((2 * ch,), lambda i: (i,), memory_space=pltpu.SMEM),
            pl.BlockSpec(memory_space=pl.ANY),
            pl.BlockSpec(memory_space=pl.ANY),
        ],
        out_specs=pl.BlockSpec(memory_space=pl.ANY),
        out_shape=jax.ShapeDtypeStruct((rows, w), hp.dtype),
        scratch_shapes=[pltpu.SemaphoreType.DMA(())],
        input_output_aliases={2: 0},
        compiler_params=_cparams(1),
        name="moe_dispatch",
    )(dest_flat, hp, xs0)


def _combine_kernel(dcur_ref, dnext_ref, ys_ref, x_ref, tbl_ref, mod_ref, ada_ref, o_ref, buf, sem, *, gate_idx):
    i = pl.program_id(0)
    n = pl.num_programs(0)
    tm = x_ref.shape[0]
    slot = i % 2

    def issue(d_ref, sl):
        def body(r, carry):
            for k in range(2):
                pltpu.make_async_copy(
                    ys_ref.at[pl.ds(d_ref[2 * r + k], 1)], buf.at[sl, k, pl.ds(r, 1)], sem.at[sl]).start()
            return carry

        lax.fori_loop(0, tm, body, 0)

    @pl.when(i == 0)
    def _():
        issue(dcur_ref, 0)

    @pl.when(i + 1 < n)
    def _():
        issue(dnext_ref, 1 - slot)

    def drain(r, carry):
        pltpu.make_async_copy(ys_ref.at[pl.ds(0, 1)], buf.at[slot, 0, pl.ds(0, 1)], sem.at[slot]).wait()
        return carry

    lax.fori_loop(0, 2 * tm, drain, 0)

    tbl = tbl_ref[...]
    w1 = tbl[:, 4:5]
    w2 = tbl[:, 5:6]
    f = w1 * buf[slot, 0] + w2 * buf[slot, 1]
    o_ref[...] = x_ref[...] + _gate_row(mod_ref, ada_ref, gate_idx) * f


def _combine(ys, dest_flat, x2, tbl, mod, ada_table, layer, s_len, gate_idx):
    t, d = x2.shape
    tm = TM_COMB
    per_b = s_len // tm
    n = t // tm
    return pl.pallas_call(
        functools.partial(_combine_kernel, gate_idx=gate_idx),
        grid=(n,),
        in_specs=[
            pl.BlockSpec((2 * tm,), lambda i: (i,), memory_space=pltpu.SMEM),
            pl.BlockSpec((2 * tm,), lambda i: (jnp.minimum(i + 1, n - 1),), memory_space=pltpu.SMEM),
            pl.BlockSpec(memory_space=pl.ANY),
            pl.BlockSpec((tm, d), lambda i: (i, 0)),
            pl.BlockSpec((tm, LANES), lambda i: (i, 0)),
            pl.BlockSpec((1, N_MOD, d), lambda i: (i // per_b, 0, 0)),
            pl.BlockSpec((None, N_MOD, d), lambda i: (layer, 0, 0)),
        ],
        out_specs=pl.BlockSpec((tm, d), lambda i: (i, 0)),
        out_shape=jax.ShapeDtypeStruct((t, d), F32),
        scratch_shapes=[
            pltpu.VMEM((2, 2, tm, d), F32),
            pltpu.SemaphoreType.DMA((2,)),
        ],
        compiler_params=_cparams(1),
        name="moe_combine",
    )(dest_flat, dest_flat, ys, x2, tbl, mod, ada_table)


def _moe_plan(tbl, cnt, tm, n_tiles):
    e = tbl[:, 0:2].astype(I32)
    pos = tbl[:, 2:4].astype(I32)
    counts = cnt[0, :N_EXPERTS].astype(I32)
    padded = ((counts + tm - 1) // tm) * tm
    ends = jnp.cumsum(padded)
    offs = ends - padded
    dest = offs[e] + pos
    tile_start = jnp.arange(n_tiles, dtype=I32) * tm
    n_used = ends[-1] // tm
    te = jnp.sum((tile_start[:, None] >= ends[None, :]).astype(I32), axis=1)
    last = jnp.minimum(jnp.maximum(n_used - 1, 0), n_tiles - 1)
    te = jnp.where(jnp.arange(n_tiles) < n_used, jnp.minimum(te, N_EXPERTS - 1), te[last])
    te = jnp.concatenate([te.astype(I32), n_used.astype(I32)[None]])
    return dest.reshape(-1), te


def kernel(x, c, w_ada, b_ada, ada_table, norm_mix_w, w_in, lb_logits, hgrn_norm_w, q_norm_w, k_norm_w,
           rel_bias, w_pool, pool_scale, w_o, norm_ffn_w, ffn_w_gate, ffn_w_up, ffn_w_down, moe_w_router,
           moe_b_router, moe_w_gate, moe_w_up, moe_w_down):
    bsz, s_len, d = x.shape
    depth = w_in.shape[0]
    t = bsz * s_len
    x2 = x.reshape(t, d)

    mod = _adaln_mod(c, w_ada, b_ada).reshape(bsz, N_MOD, d)

    lb_p = jax.nn.softmax(lb_logits.astype(F32), axis=0)
    lb_all = (jnp.cumsum(lb_p, axis=0) - lb_p[0:1]).reshape(depth, A_HEADS, 1, HEAD_DIM)
    rel = BAND_PAD + jnp.arange(CHUNK)[:, None] - jnp.arange(BAND)[None, :]
    bias = rel_bias.astype(F32)[:, jnp.clip(rel, -MAX_REL_DIST, MAX_REL_DIST) + MAX_REL_DIST]
    pool_scale3 = pool_scale.reshape(depth, 1, C_WIDTH)
    n_moe = moe_w_router.shape[0]
    w_router_pad = jnp.pad(moe_w_router, ((0, 0), (0, 0), (0, LANES - N_EXPERTS)))
    b_router_pad = jnp.pad(moe_b_router, ((0, 0), (0, LANES - N_EXPERTS))).reshape(n_moe, 1, LANES)
    tm_dense = min(TM, s_len)
    dense_te = jnp.concatenate([jnp.zeros((t // tm_dense,), I32), jnp.full((1,), t // tm_dense, I32)])

    for l in range(depth):
        h = _norm(x2, mod, ada_table, norm_mix_w, l, s_len, 0)
        proj = _in_proj(h, w_in, l, bsz, s_len)
        y_a = _hgrn(proj, lb_all[l], hgrn_norm_w[l].reshape(1, HEAD_DIM))
        y_b = _band_attention(proj, bias, q_norm_w[l].reshape(1, HEAD_DIM), k_norm_w[l].reshape(1, HEAD_DIM))
        y_c = _pool_mixer(proj, w_pool, pool_scale3, l)
        x2 = _out_proj(y_a.reshape(t, A_WIDTH), y_b.reshape(t, B_WIDTH), y_c.reshape(t, C_WIDTH),
                       w_o, x2, mod, ada_table, l, s_len, 2)
        if l % 2 == 0:
            j = l // 2
            h = _norm(x2, mod, ada_table, norm_ffn_w, l, s_len, 3)
            a = _glu(h, ffn_w_gate, ffn_w_up, dense_te, (), tm_dense, packed=False)
            x2 = _down_residual(a, ffn_w_down, j, 0, 2, x2, mod, ada_table, l, s_len, 5)
            x2 = _down_residual(a, ffn_w_down, j, 1, 2, x2, mod, ada_table, l, s_len, 5)
        else:
            j = l // 2
            hp, logits = _norm_router(x2, mod, ada_table, norm_ffn_w, w_router_pad, b_router_pad, l, j, s_len, 3)
            tbl, cnt = _route(logits)
            rows = 2 * t + N_EXPERTS * TM_E
            n_tiles = rows // TM_E
            dest, te = _moe_plan(tbl, cnt, TM_E, n_tiles)
            xs = _dispatch(hp, dest, rows)
            a = _glu(xs, moe_w_gate, moe_w_up, te, (j,), TM_E, packed=True)
            ys = _down_grouped(a, moe_w_down, j, te, TM_E)
            x2 = _combine(ys, dest, x2, tbl, mod, ada_table, l, s_len, 5)
    return x2.reshape(bsz, s_len, d)
```

```python
import functools

import jax
import jax.numpy as jnp
from jax import lax
from jax.experimental import pallas as pl
from jax.experimental.pallas import tpu as pltpu

F32 = jnp.float32
BF16 = jnp.bfloat16
U32 = jnp.uint32
I32 = jnp.int32

D_MODEL = 4096
CHUNK = 64
HEAD_DIM = 128
A_HEADS = 12
B_HEADS = 12
A_WIDTH = A_HEADS * HEAD_DIM
B_WIDTH = B_HEADS * HEAD_DIM
C_WIDTH = D_MODEL - A_WIDTH - B_WIDTH
POOL_WINDOWS = (2, 4, 8, 16)
POOL_GROUP = C_WIDTH // len(POOL_WINDOWS)
IN_COLS = 4 * A_WIDTH + 3 * B_WIDTH + C_WIDTH
IN_HEADS = IN_COLS // HEAD_DIM
BAND_CHUNKS = 9
BAND = BAND_CHUNKS * CHUNK
BAND_PAD = (BAND_CHUNKS - 1) * CHUNK
MAX_REL_DIST = 128
N_EXPERTS = 8
N_MOD = 6
EPS = 1e-6
F_MIN = 1e-6
MASK_VALUE = -1e9

LANES = 128
SUBLANES = 8
VMEM_LIMIT_MB = 56

TM = 1024
TN = 512
TN_GLU = 256
TN_GLU_E = 512
TM_E = 512
TM_NORM = 256
TM_ROUTE = 128
TM_COMB = 128
CH_DISP = 128


def _cparams(n_grid, vmem_mb=VMEM_LIMIT_MB):
    return pltpu.CompilerParams(
        dimension_semantics=("arbitrary",) * n_grid,
        vmem_limit_bytes=vmem_mb * 1024 * 1024,
    )


def _silu(x):
    return x * jax.nn.sigmoid(x)


def _cast_weight(w_ref, wbf_ref):
    k = w_ref.shape[0]
    rows = 512 if k % 512 == 0 else 128
    assert k % rows == 0

    def body(r, carry):
        sl = pl.ds(pl.multiple_of(r * rows, rows), rows)
        wbf_ref[sl, :] = w_ref[sl, :].astype(BF16)
        return carry

    lax.fori_loop(0, k // rows, body, 0)


def _mod_kernel(c_ref, w_ref, b_ref, o_ref):
    s = _silu(c_ref[...]).astype(BF16)
    w = w_ref[...].astype(BF16)
    o_ref[...] = jnp.dot(s, w, preferred_element_type=F32) + b_ref[...]


def _adaln_mod(c, w_ada, b_ada):
    bsz, d = c.shape
    n = w_ada.shape[1]
    tn = 1024
    return pl.pallas_call(
        _mod_kernel,
        grid=(n // tn,),
        in_specs=[
            pl.BlockSpec((bsz, d), lambda j: (0, 0)),
            pl.BlockSpec((d, tn), lambda j: (0, j)),
            pl.BlockSpec((1, tn), lambda j: (0, j)),
        ],
        out_specs=pl.BlockSpec((bsz, tn), lambda j: (0, j)),
        out_shape=jax.ShapeDtypeStruct((bsz, n), F32),
        compiler_params=_cparams(1),
        name="adaln_mod",
    )(c, w_ada, b_ada.reshape(1, n))


def _modulated_norm(x_ref, mod_ref, ada_ref, w_ref, shift_idx):
    x = x_ref[...]
    y = x * lax.rsqrt(jnp.mean(x * x, axis=-1, keepdims=True) + EPS) * w_ref[...]
    m = mod_ref[0] + ada_ref[...]
    shift = m[shift_idx:shift_idx + 1]
    scale = m[shift_idx + 1:shift_idx + 2]
    return y * (1.0 + scale) + shift


def _norm_kernel(x_ref, mod_ref, ada_ref, w_ref, h_ref, *, shift_idx):
    h_ref[...] = _modulated_norm(x_ref, mod_ref, ada_ref, w_ref, shift_idx).astype(BF16)


def _pack_bf16_pairs(h):
    n = h.shape[1] // 2
    lo = lax.bitcast_convert_type(h[:, :n].astype(BF16).astype(F32), U32)
    hi = lax.bitcast_convert_type(h[:, n:].astype(BF16).astype(F32), U32)
    return (hi & jnp.uint32(0xFFFF0000)) | (lo >> 16)


def _unpack_bf16_pairs(u):
    lo = lax.bitcast_convert_type(u << 16, F32).astype(BF16)
    hi = lax.bitcast_convert_type(u & jnp.uint32(0xFFFF0000), F32).astype(BF16)
    return lo, hi


def _norm_router_kernel(x_ref, mod_ref, ada_ref, w_ref, wr_ref, br_ref, hp_ref, lg_ref, *, shift_idx):
    h = _modulated_norm(x_ref, mod_ref, ada_ref, w_ref, shift_idx)
    hp_ref[...] = _pack_bf16_pairs(h)
    wr = wr_ref[...]
    h_hi = h.astype(BF16)
    h_lo = (h - h_hi.astype(F32)).astype(BF16)
    w_hi = wr.astype(BF16)
    w_lo = (wr - w_hi.astype(F32)).astype(BF16)
    lg = jnp.dot(h_hi, w_hi, preferred_element_type=F32)
    lg = lg + jnp.dot(h_hi, w_lo, preferred_element_type=F32)
    lg = lg + jnp.dot(h_lo, w_hi, preferred_element_type=F32)
    lg_ref[...] = lg + br_ref[...]


def _norm_specs(t, d, s_len, tm, layer):
    per_b = s_len // tm
    return [
        pl.BlockSpec((tm, d), lambda i: (i, 0)),
        pl.BlockSpec((1, N_MOD, d), lambda i: (i // per_b, 0, 0)),
        pl.BlockSpec((None, N_MOD, d), lambda i: (layer, 0, 0)),
        pl.BlockSpec((None, 1, d), lambda i: (layer, 0, 0)),
    ]


def _norm(x2, mod, ada_table, norm_w, layer, s_len, shift_idx):
    t, d = x2.shape
    tm = min(TM_NORM, s_len)
    return pl.pallas_call(
        functools.partial(_norm_kernel, shift_idx=shift_idx),
        grid=(t // tm,),
        in_specs=_norm_specs(t, d, s_len, tm, layer),
        out_specs=pl.BlockSpec((tm, d), lambda i: (i, 0)),
        out_shape=jax.ShapeDtypeStruct((t, d), BF16),
        compiler_params=_cparams(1),
        name="norm_mod",
    )(x2, mod, ada_table, norm_w.reshape(norm_w.shape[0], 1, d))


def _norm_router(x2, mod, ada_table, norm_w, w_router_pad, b_router_pad, layer, moe_idx, s_len, shift_idx):
    t, d = x2.shape
    tm = min(TM_NORM, s_len)
    return pl.pallas_call(
        functools.partial(_norm_router_kernel, shift_idx=shift_idx),
        grid=(t // tm,),
        in_specs=_norm_specs(t, d, s_len, tm, layer) + [
            pl.BlockSpec((None, d, LANES), lambda i: (moe_idx, 0, 0)),
            pl.BlockSpec((None, 1, LANES), lambda i: (moe_idx, 0, 0)),
        ],
        out_specs=[
            pl.BlockSpec((tm, d // 2), lambda i: (i, 0)),
            pl.BlockSpec((tm, LANES), lambda i: (i, 0)),
        ],
        out_shape=[
            jax.ShapeDtypeStruct((t, d // 2), U32),
            jax.ShapeDtypeStruct((t, LANES), F32),
        ],
        compiler_params=_cparams(1),
        name="norm_mod_router",
    )(x2, mod, ada_table, norm_w.reshape(norm_w.shape[0], 1, d), w_router_pad, b_router_pad)


def _inproj_kernel(h_ref, w_ref, o_ref, wbf_ref):
    @pl.when(pl.program_id(1) == 0)
    def _():
        _cast_weight(w_ref, wbf_ref)

    acc = jnp.dot(h_ref[...], wbf_ref[...], preferred_element_type=F32)
    for hh in range(o_ref.shape[1]):
        o_ref[0, hh] = acc[:, hh * HEAD_DIM:(hh + 1) * HEAD_DIM].astype(BF16)


def _in_proj(h, w_in, layer, bsz, s_len):
    t, d = h.shape
    n = w_in.shape[2]
    tm = min(TM, s_len)
    tn = TN
    per_b = s_len // tm
    hpt = tn // HEAD_DIM
    return pl.pallas_call(
        _inproj_kernel,
        grid=(n // tn, t // tm),
        in_specs=[
            pl.BlockSpec((tm, d), lambda j, i: (i, 0)),
            pl.BlockSpec((None, d, tn), lambda j, i: (layer, 0, j)),
        ],
        out_specs=pl.BlockSpec((1, hpt, tm, HEAD_DIM), lambda j, i: (i // per_b, j, i % per_b, 0)),
        out_shape=jax.ShapeDtypeStruct((bsz, n // HEAD_DIM, s_len, HEAD_DIM), BF16),
        scratch_shapes=[pltpu.VMEM((d, tn), BF16)],
        compiler_params=_cparams(2),
        name="in_proj",
    )(h, w_in)


_HGRN_LEVELS = (32, 16, 8, 4)
_HGRN_DIRECT = 4
_HGRN_UNROLL = 2
_ATTN_GROUP = 4
_ATTN_QROWS = _ATTN_GROUP * CHUNK
_ATTN_KROWS = BAND_PAD + _ATTN_QROWS
_ATTN_UNROLL = 2


def _row_bcast(x, period, row):
    n, c = x.shape
    x3 = x.reshape(n // period, period, c)
    return jnp.broadcast_to(x3[:, row:row + 1, :], x3.shape).reshape(n, c)


def _hgrn_kernel(q_ref, z_ref, v_ref, g_ref, lb_ref, nw_ref, o_ref, st_ref):
    s_len = q_ref.shape[2]
    nc = s_len // CHUNK
    lb = lb_ref[0]
    nw = nw_ref[...]
    row = lax.broadcasted_iota(I32, (CHUNK, HEAD_DIM), 0)
    sub3 = lax.broadcasted_iota(I32, (CHUNK // SUBLANES, SUBLANES, HEAD_DIM), 1)
    r64 = lax.broadcasted_iota(I32, (CHUNK, CHUNK), 0)
    c64 = lax.broadcasted_iota(I32, (CHUNK, CHUNK), 1)
    level_masks = []
    for hs in _HGRN_LEVELS:
        pair = 2 * hs
        same = (r64 & ~(pair - 1)) == (c64 & ~(pair - 1))
        level_masks.append(same & ((r64 & (pair - 1)) >= hs) & ((c64 & (pair - 1)) < hs))

    st_ref[...] = jnp.zeros_like(st_ref)

    def chunk(ci, carry):
        sl = pl.ds(pl.multiple_of(ci * CHUNK, CHUNK), CHUNK)
        q = q_ref[0, 0, sl, :].astype(F32)
        z = z_ref[0, 0, sl, :].astype(F32)
        v_bf = v_ref[0, 0, sl, :]
        v = v_bf.astype(F32)
        g = g_ref[0, 0, sl, :].astype(F32)

        e = jnp.exp(-jnp.abs(z))
        r = 1.0 / (1.0 + e)
        sig_pos = jnp.where(z >= 0, r, e * r)
        sig_neg = jnp.where(z >= 0, e * r, r)
        f_gate = lb + (1.0 - lb) * sig_pos
        log_f = jnp.log(jnp.maximum(f_gate, F_MIN))
        key = (1.0 - lb) * sig_neg

        grp = CHUNK // SUBLANES
        b3 = log_f.reshape(grp, SUBLANES, HEAD_DIM)
        for sh in (1, 2, 4):
            b3 = b3 + jnp.where(sub3 >= sh, pltpu.roll(b3, sh, axis=1), 0.0)
        run = jnp.zeros((1, 1, HEAD_DIM), F32)
        parts = []
        for gidx in range(grp):
            parts.append(b3[gidx:gidx + 1] + run)
            run = run + b3[gidx:gidx + 1, SUBLANES - 1:SUBLANES, :]
        b = jnp.concatenate(parts, axis=0).reshape(CHUNK, HEAD_DIM)
        b_last = run.reshape(1, HEAD_DIM)

        scores = jnp.zeros((CHUNK, CHUNK), F32)
        for hs, mask in zip(_HGRN_LEVELS, level_masks):
            ref = _row_bcast(b, 2 * hs, hs - 1)
            upper = (row & (2 * hs - 1)) >= hs
            w = jnp.exp(-jnp.abs(b - ref))
            a_l = jnp.where(upper, q * w, 0.0).astype(BF16)
            k_l = jnp.where(upper, 0.0, key * w).astype(BF16)
            s_l = lax.dot_general(a_l, k_l, (((1,), (1,)), ((), ())), preferred_element_type=F32)
            scores = scores + jnp.where(mask, s_l, 0.0)
        intra = jnp.dot(scores.astype(BF16), v_bf, preferred_element_type=F32)

        direct = jnp.sum(q * key, axis=-1, keepdims=True) * v
        for d in range(1, _HGRN_DIRECT):
            ks = pltpu.roll(key, d, axis=0)
            bs = pltpu.roll(b, d, axis=0)
            vs = pltpu.roll(v, d, axis=0)
            wgt = jnp.sum(q * ks * jnp.exp(jnp.minimum(b - bs, 0.0)), axis=-1, keepdims=True)
            direct = direct + jnp.where((row & (_HGRN_DIRECT - 1)) >= d, wgt, 0.0) * vs

        st = st_ref[...]
        qe = (q * jnp.exp(b)).astype(BF16)
        inter = lax.dot_general(qe, st.astype(BF16), (((1,), (1,)), ((), ())), preferred_element_type=F32)
        kd = (key * jnp.exp(b_last - b)).astype(BF16)
        upd = lax.dot_general(v_bf, kd, (((0,), (0,)), ((), ())), preferred_element_type=F32)
        st_ref[...] = st * jnp.exp(b_last) + upd

        out = direct + intra + inter
        y = out * lax.rsqrt(jnp.mean(out * out, axis=-1, keepdims=True) + EPS) * nw
        o_ref[0, sl, :] = (y * _silu(g)).astype(BF16)
        return carry

    lax.fori_loop(0, nc, chunk, 0, unroll=_HGRN_UNROLL)


def _hgrn(proj_hm, lb_l, norm_w_l):
    bsz, _, s_len, _ = proj_hm.shape

    def slab(off):
        return pl.BlockSpec((1, 1, s_len, HEAD_DIM), lambda b, h: (b, off + h, 0, 0))

    return pl.pallas_call(
        _hgrn_kernel,
        grid=(bsz, A_HEADS),
        in_specs=[
            slab(0), slab(A_HEADS), slab(2 * A_HEADS), slab(3 * A_HEADS),
            pl.BlockSpec((1, 1, HEAD_DIM), lambda b, h: (h, 0, 0)),
            pl.BlockSpec((1, HEAD_DIM), lambda b, h: (0, 0)),
        ],
        out_specs=pl.BlockSpec((1, s_len, HEAD_DIM), lambda b, h: (b, 0, h)),
        out_shape=jax.ShapeDtypeStruct((bsz, s_len, A_WIDTH), BF16),
        scratch_shapes=[pltpu.VMEM((HEAD_DIM, HEAD_DIM), F32)],
        compiler_params=_cparams(2),
        name="hgrn2",
    )(proj_hm, proj_hm, proj_hm, proj_hm, lb_l, norm_w_l)


def _head_rms(x, w):
    return x * lax.rsqrt(jnp.mean(x * x, axis=-1, keepdims=True) + EPS) * w


def _attn_kernel(q_ref, k_ref, v_ref, bias_ref, qw_ref, kw_ref, o_ref, kp_ref, vp_ref):
    s_len = q_ref.shape[2]
    qw = qw_ref[...]
    kw = kw_ref[...]
    scale = HEAD_DIM ** -0.5
    col = lax.broadcasted_iota(I32, (_ATTN_QROWS, _ATTN_KROWS), 1)

    kp_ref[0:BAND_PAD, :] = jnp.zeros((BAND_PAD, HEAD_DIM), BF16)
    vp_ref[0:BAND_PAD, :] = jnp.zeros((BAND_PAD, HEAD_DIM), BF16)
    blk = _ATTN_QROWS

    def prep(i, carry):
        src = pl.ds(pl.multiple_of(i * blk, blk), blk)
        dst = pl.ds(pl.multiple_of(BAND_PAD + i * blk, CHUNK), blk)
        kp_ref[dst, :] = _head_rms(k_ref[0, 0, src, :].astype(F32), kw).astype(BF16)
        vp_ref[dst, :] = v_ref[0, 0, src, :]
        return carry

    lax.fori_loop(0, s_len // blk, prep, 0, unroll=2 if (s_len // blk) % 2 == 0 else 1)

    def group(gi, carry):
        r0 = pl.multiple_of(gi * _ATTN_QROWS, _ATTN_QROWS)
        qn = _head_rms(q_ref[0, 0, pl.ds(r0, _ATTN_QROWS), :].astype(F32), qw).astype(BF16)
        kc = kp_ref[pl.ds(r0, _ATTN_KROWS), :]
        vc = vp_ref[pl.ds(r0, _ATTN_KROWS), :]
        s = lax.dot_general(qn, kc, (((1,), (1,)), ((), ())), preferred_element_type=F32) * scale + bias_ref[0]
        s = jnp.where(col + r0 >= BAND_PAD, s, MASK_VALUE)
        m = jnp.max(s, axis=-1, keepdims=True)
        p = jnp.exp(s - m)
        den = jnp.sum(p, axis=-1, keepdims=True)
        o = jnp.dot(p.astype(BF16), vc, preferred_element_type=F32) / den
        o_ref[0, pl.ds(r0, _ATTN_QROWS), :] = o.astype(BF16)
        return carry

    lax.fori_loop(0, s_len // _ATTN_QROWS, group, 0, unroll=_ATTN_UNROLL)


def _band_attention(proj_hm, bias_grp, qn_w_l, kn_w_l):
    bsz, _, s_len, _ = proj_hm.shape
    assert s_len % _ATTN_QROWS == 0
    base = 4 * A_HEADS

    def slab(off):
        return pl.BlockSpec((1, 1, s_len, HEAD_DIM), lambda h, b: (b, base + off + h, 0, 0))

    return pl.pallas_call(
        _attn_kernel,
        grid=(B_HEADS, bsz),
        in_specs=[
            slab(0), slab(B_HEADS), slab(2 * B_HEADS),
            pl.BlockSpec((1, _ATTN_QROWS, _ATTN_KROWS), lambda h, b: (h, 0, 0)),
            pl.BlockSpec((1, HEAD_DIM), lambda h, b: (0, 0)),
            pl.BlockSpec((1, HEAD_DIM), lambda h, b: (0, 0)),
        ],
        out_specs=pl.BlockSpec((1, s_len, HEAD_DIM), lambda h, b: (b, 0, h)),
        out_shape=jax.ShapeDtypeStruct((bsz, s_len, B_WIDTH), BF16),
        scratch_shapes=[
            pltpu.VMEM((BAND_PAD + s_len, HEAD_DIM), BF16),
            pltpu.VMEM((BAND_PAD + s_len, HEAD_DIM), BF16),
        ],
        compiler_params=_cparams(2),
        name="band_attention",
    )(proj_hm, proj_hm, proj_hm, bias_grp, qn_w_l, kn_w_l)


def _group_bias(rel_bias):
    rel = BAND_PAD + jnp.arange(CHUNK)[:, None] - jnp.arange(BAND)[None, :]
    bias = rel_bias.astype(F32)[:, jnp.clip(rel, -MAX_REL_DIST, MAX_REL_DIST) + MAX_REL_DIST]
    out = jnp.full((rel_bias.shape[0], _ATTN_QROWS, _ATTN_KROWS), MASK_VALUE, F32)
    for a in range(_ATTN_GROUP):
        out = out.at[:, a * CHUNK:(a + 1) * CHUNK, a * CHUNK:a * CHUNK + BAND].set(bias)
    return out


_POOL_HALO = 16


def _pool_kernel(p_ref, w_ref, sc_ref, o_ref, pad_ref, wbf_ref):
    s_len = p_ref.shape[2]
    gi = pl.program_id(1)
    blk = 256 if s_len % 256 == 0 else CHUNK
    nblk = s_len // blk
    hp = p_ref.shape[1]
    pad_ref[0:_POOL_HALO, :] = jnp.zeros((_POOL_HALO, pad_ref.shape[1]), F32)
    for hh in range(hp):
        pad_ref[_POOL_HALO:_POOL_HALO + s_len, hh * HEAD_DIM:(hh + 1) * HEAD_DIM] = p_ref[0, hh].astype(F32)
    wbf_ref[...] = w_ref[0].astype(BF16)
    scale = sc_ref[...]

    for g, win in enumerate(POOL_WINDOWS):
        @pl.when(gi == g)
        def _(win=win):
            def body(i, carry):
                r0 = pl.multiple_of(i * blk, blk)
                acc = pad_ref[pl.ds(r0, _POOL_HALO + blk), :]
                cur = acc[_POOL_HALO:, :]
                span = 1
                while span < win:
                    acc = acc + pltpu.roll(acc, span, axis=0)
                    span *= 2
                acc = acc[_POOL_HALO:, :]
                pos = lax.broadcasted_iota(I32, acc.shape, 0) + r0 + 1
                cnt = jnp.minimum(pos, win).astype(F32)
                mixed = (acc / cnt - cur).astype(BF16)
                out = jnp.dot(mixed, wbf_ref[...], preferred_element_type=F32) * scale
                o_ref[0, pl.ds(r0, blk), :] = out.astype(BF16)
                return carry

            lax.fori_loop(0, nblk, body, 0)


def _pool_mixer(proj_hm, w_pool, pool_scale3, layer):
    bsz, _, s_len, _ = proj_hm.shape
    hp = POOL_GROUP // HEAD_DIM
    base = (4 * A_HEADS + 3 * B_HEADS) // hp
    n_grp = len(POOL_WINDOWS)
    return pl.pallas_call(
        _pool_kernel,
        grid=(bsz, n_grp),
        in_specs=[
            pl.BlockSpec((1, hp, s_len, HEAD_DIM), lambda b, g: (b, base + g, 0, 0)),
            pl.BlockSpec((None, 1, POOL_GROUP, POOL_GROUP), lambda b, g: (layer, g, 0, 0)),
            pl.BlockSpec((None, 1, POOL_GROUP), lambda b, g: (layer, 0, g)),
        ],
        out_specs=pl.BlockSpec((1, s_len, POOL_GROUP), lambda b, g: (b, 0, g)),
        out_shape=jax.ShapeDtypeStruct((bsz, s_len, C_WIDTH), BF16),
        scratch_shapes=[
            pltpu.VMEM((_POOL_HALO + s_len, POOL_GROUP), F32),
            pltpu.VMEM((POOL_GROUP, POOL_GROUP), BF16),
        ],
        compiler_params=_cparams(2),
        name="pool_mixer",
    )(proj_hm, w_pool, pool_scale3)


def _gate_row(mod_ref, ada_ref, gate_idx):
    return mod_ref[0, gate_idx:gate_idx + 1, :] + ada_ref[gate_idx:gate_idx + 1, :]


def _outproj_kernel(ya_ref, yb_ref, yc_ref, w_ref, x_ref, mod_ref, ada_ref, o_ref, wbf_ref, *, gate_idx):
    @pl.when(pl.program_id(1) == 0)
    def _():
        _cast_weight(w_ref, wbf_ref)

    acc = jnp.dot(ya_ref[...], wbf_ref[0:A_WIDTH, :], preferred_element_type=F32)
    acc = acc + jnp.dot(yb_ref[...], wbf_ref[A_WIDTH:A_WIDTH + B_WIDTH, :], preferred_element_type=F32)
    acc = acc + jnp.dot(yc_ref[...], wbf_ref[A_WIDTH + B_WIDTH:, :], preferred_element_type=F32)
    o_ref[...] = x_ref[...] + _gate_row(mod_ref, ada_ref, gate_idx) * acc


def _out_proj(ya, yb, yc, w_o, x2, mod, ada_table, layer, s_len, gate_idx):
    t, d = x2.shape
    tm = min(TM, s_len)
    tn = TN
    per_b = s_len // tm
    return pl.pallas_call(
        functools.partial(_outproj_kernel, gate_idx=gate_idx),
        grid=(d // tn, t // tm),
        in_specs=[
            pl.BlockSpec((tm, A_WIDTH), lambda j, i: (i, 0)),
            pl.BlockSpec((tm, B_WIDTH), lambda j, i: (i, 0)),
            pl.BlockSpec((tm, C_WIDTH), lambda j, i: (i, 0)),
            pl.BlockSpec((None, d, tn), lambda j, i: (layer, 0, j)),
            pl.BlockSpec((tm, tn), lambda j, i: (i, j)),
            pl.BlockSpec((1, N_MOD, tn), lambda j, i: (i // per_b, 0, j)),
            pl.BlockSpec((None, N_MOD, tn), lambda j, i: (layer, 0, j)),
        ],
        out_specs=pl.BlockSpec((tm, tn), lambda j, i: (i, j)),
        out_shape=jax.ShapeDtypeStruct((t, d), F32),
        scratch_shapes=[pltpu.VMEM((d, tn), BF16)],
        compiler_params=_cparams(2),
        name="out_proj",
    )(ya, yb, yc, w_o, x2, mod, ada_table)


def _tile_flags(te_ref, n_tiles):
    i = pl.program_id(1)
    prev = te_ref[jnp.maximum(i - 1, 0)]
    first = jnp.logical_or(i == 0, te_ref[i] != prev)
    valid = i < te_ref[n_tiles]
    return first, valid


def _glu_kernel(te_ref, x_ref, wg_ref, wu_ref, o_ref, wg_bf, wu_bf, *, n_tiles, packed):
    first, valid = _tile_flags(te_ref, n_tiles)

    @pl.when(first)
    def _():
        _cast_weight(wg_ref, wg_bf)
        _cast_weight(wu_ref, wu_bf)

    @pl.when(valid)
    def _():
        if packed:
            half = x_ref.shape[1]
            lo, hi = _unpack_bf16_pairs(x_ref[...])
            g = jnp.dot(lo, wg_bf[0:half, :], preferred_element_type=F32)
            g = g + jnp.dot(hi, wg_bf[half:, :], preferred_element_type=F32)
            u = jnp.dot(lo, wu_bf[0:half, :], preferred_element_type=F32)
            u = u + jnp.dot(hi, wu_bf[half:, :], preferred_element_type=F32)
        else:
            x = x_ref[...]
            g = jnp.dot(x, wg_bf[...], preferred_element_type=F32)
            u = jnp.dot(x, wu_bf[...], preferred_element_type=F32)
        o_ref[...] = (_silu(g) * u).astype(BF16)

    @pl.when(jnp.logical_not(valid))
    def _():
        o_ref[...] = jnp.zeros_like(o_ref)


def _glu(x, wg, wu, te, widx, tm, tn, packed):
    rows, xc = x.shape
    d, f = wg.shape[-2], wg.shape[-1]
    n_tiles = rows // tm
    lead = len(wg.shape) - 3

    def w_map(j, i, te_ref):
        return tuple(widx[:lead]) + (te_ref[i], 0, j)

    w_block = (None,) * (lead + 1) + (d, tn)
    grid_spec = pltpu.PrefetchScalarGridSpec(
        num_scalar_prefetch=1,
        grid=(f // tn, n_tiles),
        in_specs=[
            pl.BlockSpec((tm, xc), lambda j, i, te_ref: (i, 0)),
            pl.BlockSpec(w_block, w_map),
            pl.BlockSpec(w_block, w_map),
        ],
        out_specs=pl.BlockSpec((tm, tn), lambda j, i, te_ref: (i, j)),
        scratch_shapes=[pltpu.VMEM((d, tn), BF16), pltpu.VMEM((d, tn), BF16)],
    )
    return pl.pallas_call(
        functools.partial(_glu_kernel, n_tiles=n_tiles, packed=packed),
        grid_spec=grid_spec,
        out_shape=jax.ShapeDtypeStruct((rows, f), BF16),
        compiler_params=_cparams(2),
        name="glu_pair",
    )(te, x, wg, wu)


def _down_res_kernel(a_ref, w_ref, r_ref, mod_ref, ada_ref, o_ref, wbf_ref, *, gate_idx):
    @pl.when(pl.program_id(1) == 0)
    def _():
        _cast_weight(w_ref, wbf_ref)

    acc = jnp.dot(a_ref[...], wbf_ref[...], preferred_element_type=F32)
    o_ref[...] = r_ref[...] + _gate_row(mod_ref, ada_ref, gate_idx) * acc


def _down_residual(a, w_down, widx, k_half, n_half, res, mod, ada_table, layer, s_len, gate_idx):
    t, d = res.shape
    kh = a.shape[1] // n_half
    tm = min(TM_E, s_len)
    tn = TN
    per_b = s_len // tm
    return pl.pallas_call(
        functools.partial(_down_res_kernel, gate_idx=gate_idx),
        grid=(d // tn, t // tm),
        in_specs=[
            pl.BlockSpec((tm, kh), lambda j, i: (i, k_half)),
            pl.BlockSpec((None, kh, tn), lambda j, i: (widx, k_half, j)),
            pl.BlockSpec((tm, tn), lambda j, i: (i, j)),
            pl.BlockSpec((1, N_MOD, tn), lambda j, i: (i // per_b, 0, j)),
            pl.BlockSpec((None, N_MOD, tn), lambda j, i: (layer, 0, j)),
        ],
        out_specs=pl.BlockSpec((tm, tn), lambda j, i: (i, j)),
        out_shape=jax.ShapeDtypeStruct((t, d), F32),
        scratch_shapes=[pltpu.VMEM((kh, tn), BF16)],
        compiler_params=_cparams(2),
        name="down_residual",
    )(a, w_down, res, mod, ada_table)


def _down_group_kernel(te_ref, a_ref, w_ref, o_ref, wbf_ref, *, n_tiles):
    first, valid = _tile_flags(te_ref, n_tiles)

    @pl.when(first)
    def _():
        _cast_weight(w_ref, wbf_ref)

    @pl.when(valid)
    def _():
        o_ref[...] = jnp.dot(a_ref[...], wbf_ref[...], preferred_element_type=F32)

    @pl.when(jnp.logical_not(valid))
    def _():
        o_ref[...] = jnp.zeros_like(o_ref)


def _down_grouped(a, w_down, moe_idx, te, tm):
    rows, k = a.shape
    d = w_down.shape[-1]
    tn = TN
    n_tiles = rows // tm
    grid_spec = pltpu.PrefetchScalarGridSpec(
        num_scalar_prefetch=1,
        grid=(d // tn, n_tiles),
        in_specs=[
            pl.BlockSpec((tm, k), lambda j, i, te_ref: (i, 0)),
            pl.BlockSpec((None, None, k, tn), lambda j, i, te_ref: (moe_idx, te_ref[i], 0, j)),
        ],
        out_specs=pl.BlockSpec((tm, tn), lambda j, i, te_ref: (i, j)),
        scratch_shapes=[pltpu.VMEM((k, tn), BF16)],
    )
    return pl.pallas_call(
        functools.partial(_down_group_kernel, n_tiles=n_tiles),
        grid_spec=grid_spec,
        out_shape=jax.ShapeDtypeStruct((rows, d), F32),
        compiler_params=_cparams(2),
        name="down_grouped",
    )(te, a, w_down)


def _route_kernel(lg_ref, tbl_ref, cnt_ref, carry_ref):
    i = pl.program_id(0)

    @pl.when(i == 0)
    def _():
        carry_ref[...] = jnp.zeros_like(carry_ref)

    tm = lg_ref.shape[0]
    lane = lax.broadcasted_iota(I32, (tm, LANES), 1).astype(F32)
    neg = jnp.float32(-jnp.inf)
    lg = jnp.where(lane < N_EXPERTS, lg_ref[...], neg)
    m1 = jnp.max(lg, axis=-1, keepdims=True)
    e1 = jnp.min(jnp.where(lg == m1, lane, float(LANES)), axis=-1, keepdims=True)
    lg2 = jnp.where(lane == e1, neg, lg)
    m2 = jnp.max(lg2, axis=-1, keepdims=True)
    e2 = jnp.min(jnp.where(lg2 == m2, lane, float(LANES)), axis=-1, keepdims=True)
    tt = jnp.exp(m2 - m1)
    w1 = 1.0 / (1.0 + tt)
    w2 = tt * w1

    memb = jnp.logical_or(lane == e1, lane == e2).astype(F32)
    rr = lax.broadcasted_iota(I32, (tm, tm), 0)
    cc = lax.broadcasted_iota(I32, (tm, tm), 1)
    strict_lower = (cc < rr).astype(BF16)
    pos = jnp.dot(strict_lower, memb.astype(BF16), preferred_element_type=F32) + carry_ref[...]
    p1 = jnp.sum(jnp.where(lane == e1, pos, 0.0), axis=-1, keepdims=True)
    p2 = jnp.sum(jnp.where(lane == e2, pos, 0.0), axis=-1, keepdims=True)
    carry_ref[...] = carry_ref[...] + jnp.sum(memb, axis=0, keepdims=True)

    out = jnp.where(lane == 0, e1, 0.0)
    out = jnp.where(lane == 1, e2, out)
    out = jnp.where(lane == 2, p1, out)
    out = jnp.where(lane == 3, p2, out)
    out = jnp.where(lane == 4, w1, out)
    out = jnp.where(lane == 5, w2, out)
    tbl_ref[...] = out
    cnt_ref[...] = carry_ref[...]


def _route(logits):
    t = logits.shape[0]
    tm = TM_ROUTE
    return pl.pallas_call(
        _route_kernel,
        grid=(t // tm,),
        in_specs=[pl.BlockSpec((tm, LANES), lambda i: (i, 0))],
        out_specs=[
            pl.BlockSpec((tm, LANES), lambda i: (i, 0)),
            pl.BlockSpec((1, LANES), lambda i: (0, 0)),
        ],
        out_shape=[
            jax.ShapeDtypeStruct((t, LANES), F32),
            jax.ShapeDtypeStruct((1, LANES), F32),
        ],
        scratch_shapes=[pltpu.VMEM((1, LANES), F32)],
        compiler_params=_cparams(1),
        name="moe_route",
    )(logits)


def _dispatch_kernel(dest_ref, hp_ref, xs_in_ref, xs_ref, sem):
    del xs_in_ref
    ch = hp_ref.shape[0]

    def issue(r, carry):
        src = hp_ref.at[pl.ds(r, 1)]
        pltpu.make_async_copy(src, xs_ref.at[pl.ds(dest_ref[2 * r], 1)], sem).start()
        pltpu.make_async_copy(src, xs_ref.at[pl.ds(dest_ref[2 * r + 1], 1)], sem).start()
        return carry

    lax.fori_loop(0, ch, issue, 0, unroll=8)

    def drain(r, carry):
        pltpu.make_async_copy(hp_ref.at[pl.ds(0, 1)], xs_ref.at[pl.ds(0, 1)], sem).wait()
        return carry

    lax.fori_loop(0, 2 * ch, drain, 0)


def _dispatch(hp, dest_flat, rows):
    t, w = hp.shape
    ch = CH_DISP
    xs0 = jnp.zeros((rows, w), hp.dtype)
    return pl.pallas_call(
        _dispatch_kernel,
        grid=(t // ch,),
        in_specs=[
            pl.BlockSpec((2 * ch,), lambda i: (i,), memory_space=pltpu.SMEM),
            pl.BlockSpec((ch, w), lambda i: (i, 0)),
            pl.BlockSpec(memory_space=pl.ANY),
        ],
        out_specs=pl.BlockSpec(memory_space=pl.ANY),
        out_shape=jax.ShapeDtypeStruct((rows, w), hp.dtype),
        scratch_shapes=[pltpu.SemaphoreType.DMA(())],
        input_output_aliases={2: 0},
        compiler_params=_cparams(1),
        name="moe_dispatch",
    )(dest_flat, hp, xs0)


def _combine_kernel(dcur_ref, dnext_ref, ys_ref, x_ref, tbl_ref, mod_ref, ada_ref, o_ref, buf, sem, *, gate_idx):
    i = pl.program_id(0)
    n = pl.num_programs(0)
    tm = x_ref.shape[0]
    slot = i % 2

    def issue(d_ref, sl):
        def body(r, carry):
            for k in range(2):
                pltpu.make_async_copy(
                    ys_ref.at[pl.ds(d_ref[2 * r + k], 1)], buf.at[sl, k, pl.ds(r, 1)], sem.at[sl]).start()
            return carry

        lax.fori_loop(0, tm, body, 0)

    @pl.when(i == 0)
    def _():
        issue(dcur_ref, 0)

    @pl.when(i + 1 < n)
    def _():
        issue(dnext_ref, 1 - slot)

    def drain(r, carry):
        pltpu.make_async_copy(ys_ref.at[pl.ds(0, 1)], buf.at[slot, 0, pl.ds(0, 1)], sem.at[slot]).wait()
        return carry

    lax.fori_loop(0, 2 * tm, drain, 0)

    tbl = tbl_ref[...]
    w1 = tbl[:, 4:5]
    w2 = tbl[:, 5:6]
    f = w1 * buf[slot, 0] + w2 * buf[slot, 1]
    o_ref[...] = x_ref[...] + _gate_row(mod_ref, ada_ref, gate_idx) * f


def _combine(ys, dest_flat, x2, tbl, mod, ada_table, layer, s_len, gate_idx):
    t, d = x2.shape
    tm = TM_COMB
    per_b = s_len // tm
    n = t // tm
    return pl.pallas_call(
        functools.partial(_combine_kernel, gate_idx=gate_idx),
        grid=(n,),
        in_specs=[
            pl.BlockSpec((2 * tm,), lambda i: (i,), memory_space=pltpu.SMEM),
            pl.BlockSpec((2 * tm,), lambda i: (jnp.minimum(i + 1, n - 1),), memory_space=pltpu.SMEM),
            pl.BlockSpec(memory_space=pl.ANY),
            pl.BlockSpec((tm, d), lambda i: (i, 0)),
            pl.BlockSpec((tm, LANES), lambda i: (i, 0)),
            pl.BlockSpec((1, N_MOD, d), lambda i: (i // per_b, 0, 0)),
            pl.BlockSpec((None, N_MOD, d), lambda i: (layer, 0, 0)),
        ],
        out_specs=pl.BlockSpec((tm, d), lambda i: (i, 0)),
        out_shape=jax.ShapeDtypeStruct((t, d), F32),
        scratch_shapes=[
            pltpu.VMEM((2, 2, tm, d), F32),
            pltpu.SemaphoreType.DMA((2,)),
        ],
        compiler_params=_cparams(1),
        name="moe_combine",
    )(dest_flat, dest_flat, ys, x2, tbl, mod, ada_table)


def _moe_plan(tbl, cnt, tm, n_tiles):
    e = tbl[:, 0:2].astype(I32)
    pos = tbl[:, 2:4].astype(I32)
    counts = cnt[0, :N_EXPERTS].astype(I32)
    padded = ((counts + tm - 1) // tm) * tm
    ends = jnp.cumsum(padded)
    offs = ends - padded
    expert_ids = jnp.arange(N_EXPERTS, dtype=I32)
    dest = jnp.sum(jnp.where(e[..., None] == expert_ids, offs, 0), axis=-1) + pos
    tile_start = jnp.arange(n_tiles, dtype=I32) * tm
    n_used = ends[-1] // tm
    te = jnp.sum((tile_start[:, None] >= ends[None, :]).astype(I32), axis=1)
    last = jnp.minimum(jnp.maximum(n_used - 1, 0), n_tiles - 1)
    te = jnp.where(jnp.arange(n_tiles) < n_used, jnp.minimum(te, N_EXPERTS - 1), te[last])
    te = jnp.concatenate([te.astype(I32), n_used.astype(I32)[None]])
    return dest.reshape(-1), te


def kernel(x, c, w_ada, b_ada, ada_table, norm_mix_w, w_in, lb_logits, hgrn_norm_w, q_norm_w, k_norm_w,
           rel_bias, w_pool, pool_scale, w_o, norm_ffn_w, ffn_w_gate, ffn_w_up, ffn_w_down, moe_w_router,
           moe_b_router, moe_w_gate, moe_w_up, moe_w_down):
    bsz, s_len, d = x.shape
    depth = w_in.shape[0]
    t = bsz * s_len
    x2 = x.reshape(t, d)

    mod = _adaln_mod(c, w_ada, b_ada).reshape(bsz, N_MOD, d)

    lb_p = jax.nn.softmax(lb_logits.astype(F32), axis=0)
    lb_all = (jnp.cumsum(lb_p, axis=0) - lb_p[0:1]).reshape(depth, A_HEADS, 1, HEAD_DIM)
    bias = _group_bias(rel_bias)
    pool_scale3 = pool_scale.reshape(depth, 1, C_WIDTH)
    n_moe = moe_w_router.shape[0]
    w_router_pad = jnp.pad(moe_w_router, ((0, 0), (0, 0), (0, LANES - N_EXPERTS)))
    b_router_pad = jnp.pad(moe_b_router, ((0, 0), (0, LANES - N_EXPERTS))).reshape(n_moe, 1, LANES)
    tm_dense = min(TM, s_len)
    dense_te = jnp.concatenate([jnp.zeros((t // tm_dense,), I32), jnp.full((1,), t // tm_dense, I32)])

    for l in range(depth):
        h = _norm(x2, mod, ada_table, norm_mix_w, l, s_len, 0)
        proj = _in_proj(h, w_in, l, bsz, s_len)
        y_a = _hgrn(proj, lb_all[l], hgrn_norm_w[l].reshape(1, HEAD_DIM))
        y_b = _band_attention(proj, bias, q_norm_w[l].reshape(1, HEAD_DIM), k_norm_w[l].reshape(1, HEAD_DIM))
        y_c = _pool_mixer(proj, w_pool, pool_scale3, l)
        x2 = _out_proj(y_a.reshape(t, A_WIDTH), y_b.reshape(t, B_WIDTH), y_c.reshape(t, C_WIDTH),
                       w_o, x2, mod, ada_table, l, s_len, 2)
        if l % 2 == 0:
            j = l // 2
            h = _norm(x2, mod, ada_table, norm_ffn_w, l, s_len, 3)
            a = _glu(h, ffn_w_gate, ffn_w_up, dense_te, (), tm_dense, TN_GLU, packed=False)
            x2 = _down_residual(a, ffn_w_down, j, 0, 2, x2, mod, ada_table, l, s_len, 5)
            x2 = _down_residual(a, ffn_w_down, j, 1, 2, x2, mod, ada_table, l, s_len, 5)
        else:
            j = l // 2
            hp, logits = _norm_router(x2, mod, ada_table, norm_ffn_w, w_router_pad, b_router_pad, l, j, s_len, 3)
            tbl, cnt = _route(logits)
            rows = 2 * t + N_EXPERTS * TM_E
            n_tiles = rows // TM_E
            dest, te = _moe_plan(tbl, cnt, TM_E, n_tiles)
            xs = _dispatch(hp, dest, rows)
            a = _glu(xs, moe_w_gate, moe_w_up, te, (j,), TM_E, TN_GLU_E, packed=True)
            ys = _down_grouped(a, moe_w_down, j, te, TM_E)
            x2 = _combine(ys, dest, x2, tbl, mod, ada_table, l, s_len, 5)
    return x2.reshape(bsz, s_len, d)
```

```python
import functools

import jax
import jax.numpy as jnp
from jax import lax
from jax.experimental import pallas as pl
from jax.experimental.pallas import tpu as pltpu

F32 = jnp.float32
BF16 = jnp.bfloat16
U32 = jnp.uint32
I32 = jnp.int32

D_MODEL = 4096
CHUNK = 64
HEAD_DIM = 128
A_HEADS = 12
B_HEADS = 12
A_WIDTH = A_HEADS * HEAD_DIM
B_WIDTH = B_HEADS * HEAD_DIM
C_WIDTH = D_MODEL - A_WIDTH - B_WIDTH
POOL_WINDOWS = (2, 4, 8, 16)
POOL_GROUP = C_WIDTH // len(POOL_WINDOWS)
IN_COLS = 4 * A_WIDTH + 3 * B_WIDTH + C_WIDTH
IN_HEADS = IN_COLS // HEAD_DIM
BAND_CHUNKS = 9
BAND = BAND_CHUNKS * CHUNK
BAND_PAD = (BAND_CHUNKS - 1) * CHUNK
MAX_REL_DIST = 128
N_EXPERTS = 8
N_MOD = 6
EPS = 1e-6
F_MIN = 1e-6
MASK_VALUE = -1e9

LANES = 128
SUBLANES = 8
VMEM_LIMIT_MB = 56

TM = 1024
TN = 512
TN_GLU = 256
TN_GLU_E = 512
TM_E = 512
TM_NORM = 512
TM_ROUTE = 128
TM_COMB = 256
CH_DISP = 128


def _cparams(n_grid, vmem_mb=VMEM_LIMIT_MB):
    return pltpu.CompilerParams(
        dimension_semantics=("arbitrary",) * n_grid,
        vmem_limit_bytes=vmem_mb * 1024 * 1024,
    )


def _silu(x):
    return x * jax.nn.sigmoid(x)


def _cast_weight(w_ref, wbf_ref):
    k = w_ref.shape[0]
    rows = 512 if k % 512 == 0 else 128
    assert k % rows == 0

    def body(r, carry):
        sl = pl.ds(pl.multiple_of(r * rows, rows), rows)
        wbf_ref[sl, :] = w_ref[sl, :].astype(BF16)
        return carry

    lax.fori_loop(0, k // rows, body, 0)


def _mod_kernel(c_ref, w_ref, b_ref, o_ref):
    s = _silu(c_ref[...]).astype(BF16)
    w = w_ref[...].astype(BF16)
    o_ref[...] = jnp.dot(s, w, preferred_element_type=F32) + b_ref[...]


def _adaln_mod(c, w_ada, b_ada):
    bsz, d = c.shape
    n = w_ada.shape[1]
    tn = 1024
    return pl.pallas_call(
        _mod_kernel,
        grid=(n // tn,),
        in_specs=[
            pl.BlockSpec((bsz, d), lambda j: (0, 0)),
            pl.BlockSpec((d, tn), lambda j: (0, j)),
            pl.BlockSpec((1, tn), lambda j: (0, j)),
        ],
        out_specs=pl.BlockSpec((bsz, tn), lambda j: (0, j)),
        out_shape=jax.ShapeDtypeStruct((bsz, n), F32),
        compiler_params=_cparams(1),
        name="adaln_mod",
    )(c, w_ada, b_ada.reshape(1, n))


def _modulated_norm(x_ref, mod_ref, ada_ref, w_ref, shift_idx):
    x = x_ref[...]
    y = x * lax.rsqrt(jnp.mean(x * x, axis=-1, keepdims=True) + EPS) * w_ref[...]
    m = mod_ref[0] + ada_ref[...]
    shift = m[shift_idx:shift_idx + 1]
    scale = m[shift_idx + 1:shift_idx + 2]
    return y * (1.0 + scale) + shift


def _norm_kernel(x_ref, mod_ref, ada_ref, w_ref, h_ref, *, shift_idx):
    h_ref[...] = _modulated_norm(x_ref, mod_ref, ada_ref, w_ref, shift_idx).astype(BF16)


def _pack_bf16_pairs(h):
    n = h.shape[1] // 2
    lo = lax.bitcast_convert_type(h[:, :n].astype(BF16).astype(F32), U32)
    hi = lax.bitcast_convert_type(h[:, n:].astype(BF16).astype(F32), U32)
    return (hi & jnp.uint32(0xFFFF0000)) | (lo >> 16)


def _unpack_bf16_pairs(u):
    lo = lax.bitcast_convert_type(u << 16, F32).astype(BF16)
    hi = lax.bitcast_convert_type(u & jnp.uint32(0xFFFF0000), F32).astype(BF16)
    return lo, hi


def _norm_router_kernel(x_ref, mod_ref, ada_ref, w_ref, wr_ref, br_ref, hp_ref, lg_ref, *, shift_idx):
    h = _modulated_norm(x_ref, mod_ref, ada_ref, w_ref, shift_idx)
    hp_ref[...] = _pack_bf16_pairs(h)
    wr = wr_ref[...]
    h_hi = h.astype(BF16)
    h_lo = (h - h_hi.astype(F32)).astype(BF16)
    w_hi = wr.astype(BF16)
    w_lo = (wr - w_hi.astype(F32)).astype(BF16)
    lg = jnp.dot(h_hi, w_hi, preferred_element_type=F32)
    lg = lg + jnp.dot(h_hi, w_lo, preferred_element_type=F32)
    lg = lg + jnp.dot(h_lo, w_hi, preferred_element_type=F32)
    lg_ref[...] = lg + br_ref[...]


def _norm_specs(t, d, s_len, tm, layer):
    per_b = s_len // tm
    return [
        pl.BlockSpec((tm, d), lambda i: (i, 0)),
        pl.BlockSpec((1, N_MOD, d), lambda i: (i // per_b, 0, 0)),
        pl.BlockSpec((None, N_MOD, d), lambda i: (layer, 0, 0)),
        pl.BlockSpec((None, 1, d), lambda i: (layer, 0, 0)),
    ]


def _norm(x2, mod, ada_table, norm_w, layer, s_len, shift_idx):
    t, d = x2.shape
    tm = min(TM_NORM, s_len)
    return pl.pallas_call(
        functools.partial(_norm_kernel, shift_idx=shift_idx),
        grid=(t // tm,),
        in_specs=_norm_specs(t, d, s_len, tm, layer),
        out_specs=pl.BlockSpec((tm, d), lambda i: (i, 0)),
        out_shape=jax.ShapeDtypeStruct((t, d), BF16),
        compiler_params=_cparams(1),
        name="norm_mod",
    )(x2, mod, ada_table, norm_w.reshape(norm_w.shape[0], 1, d))


def _norm_router(x2, mod, ada_table, norm_w, w_router_pad, b_router_pad, layer, moe_idx, s_len, shift_idx):
    t, d = x2.shape
    tm = min(TM_NORM, s_len)
    return pl.pallas_call(
        functools.partial(_norm_router_kernel, shift_idx=shift_idx),
        grid=(t // tm,),
        in_specs=_norm_specs(t, d, s_len, tm, layer) + [
            pl.BlockSpec((None, d, LANES), lambda i: (moe_idx, 0, 0)),
            pl.BlockSpec((None, 1, LANES), lambda i: (moe_idx, 0, 0)),
        ],
        out_specs=[
            pl.BlockSpec((tm, d // 2), lambda i: (i, 0)),
            pl.BlockSpec((tm, LANES), lambda i: (i, 0)),
        ],
        out_shape=[
            jax.ShapeDtypeStruct((t, d // 2), U32),
            jax.ShapeDtypeStruct((t, LANES), F32),
        ],
        compiler_params=_cparams(1),
        name="norm_mod_router",
    )(x2, mod, ada_table, norm_w.reshape(norm_w.shape[0], 1, d), w_router_pad, b_router_pad)


def _inproj_kernel(h_ref, w_ref, o_ref, wbf_ref):
    @pl.when(pl.program_id(1) == 0)
    def _():
        _cast_weight(w_ref, wbf_ref)

    acc = jnp.dot(h_ref[...], wbf_ref[...], preferred_element_type=F32)
    for hh in range(o_ref.shape[1]):
        o_ref[0, hh] = acc[:, hh * HEAD_DIM:(hh + 1) * HEAD_DIM].astype(BF16)


def _in_proj(h, w_in, layer, bsz, s_len):
    t, d = h.shape
    n = w_in.shape[2]
    tm = min(TM, s_len)
    tn = TN
    per_b = s_len // tm
    hpt = tn // HEAD_DIM
    return pl.pallas_call(
        _inproj_kernel,
        grid=(n // tn, t // tm),
        in_specs=[
            pl.BlockSpec((tm, d), lambda j, i: (i, 0)),
            pl.BlockSpec((None, d, tn), lambda j, i: (layer, 0, j)),
        ],
        out_specs=pl.BlockSpec((1, hpt, tm, HEAD_DIM), lambda j, i: (i // per_b, j, i % per_b, 0)),
        out_shape=jax.ShapeDtypeStruct((bsz, n // HEAD_DIM, s_len, HEAD_DIM), BF16),
        scratch_shapes=[pltpu.VMEM((d, tn), BF16)],
        compiler_params=_cparams(2),
        name="in_proj",
    )(h, w_in)


_HGRN_LEVELS = (32, 16, 8, 4, 2, 1)
_HGRN_UNROLL = 4
_ATTN_GROUP = 4
_ATTN_QROWS = _ATTN_GROUP * CHUNK
_ATTN_KROWS = BAND_PAD + _ATTN_QROWS
_ATTN_UNROLL = 2


def _row_bcast(x, period, row):
    n, c = x.shape
    x3 = x.reshape(n // period, period, c)
    return jnp.broadcast_to(x3[:, row:row + 1, :], x3.shape).reshape(n, c)


def _hgrn_kernel(q_ref, z_ref, v_ref, g_ref, lb_ref, nw_ref, o_ref, st_ref):
    s_len = q_ref.shape[2]
    nc = s_len // CHUNK
    lb = lb_ref[0]
    nw = nw_ref[...]
    row = lax.broadcasted_iota(I32, (CHUNK, HEAD_DIM), 0)
    sub3 = lax.broadcasted_iota(I32, (CHUNK // SUBLANES, SUBLANES, HEAD_DIM), 1)
    r64 = lax.broadcasted_iota(I32, (CHUNK, CHUNK), 0)
    c64 = lax.broadcasted_iota(I32, (CHUNK, CHUNK), 1)
    level_masks = []
    for hs in _HGRN_LEVELS:
        pair = 2 * hs
        same = (r64 & ~(pair - 1)) == (c64 & ~(pair - 1))
        level_masks.append(same & ((r64 & (pair - 1)) >= hs) & ((c64 & (pair - 1)) < hs))

    st_ref[...] = jnp.zeros_like(st_ref)

    def chunk(ci, carry):
        sl = pl.ds(pl.multiple_of(ci * CHUNK, CHUNK), CHUNK)
        q_bf = q_ref[0, 0, sl, :]
        q = q_bf.astype(F32)
        z = z_ref[0, 0, sl, :].astype(F32)
        v_bf = v_ref[0, 0, sl, :]
        g = g_ref[0, 0, sl, :].astype(F32)

        e = jnp.exp(-jnp.abs(z))
        r = 1.0 / (1.0 + e)
        sig_pos = jnp.where(z >= 0, r, e * r)
        sig_neg = jnp.where(z >= 0, e * r, r)
        f_gate = lb + (1.0 - lb) * sig_pos
        log_f = jnp.log2(jnp.maximum(f_gate, F_MIN))
        key = (1.0 - lb) * sig_neg

        grp = CHUNK // SUBLANES
        b3 = log_f.reshape(grp, SUBLANES, HEAD_DIM)
        for sh in (1, 2, 4):
            b3 = b3 + jnp.where(sub3 >= sh, pltpu.roll(b3, sh, axis=1), 0.0)
        run = jnp.zeros((1, 1, HEAD_DIM), F32)
        parts = []
        for gidx in range(grp):
            parts.append(b3[gidx:gidx + 1] + run)
            run = run + b3[gidx:gidx + 1, SUBLANES - 1:SUBLANES, :]
        b = jnp.concatenate(parts, axis=0).reshape(CHUNK, HEAD_DIM)
        b_last = run.reshape(1, HEAD_DIM)

        m4 = row & 3
        key_bf = key.astype(BF16)
        scores = jnp.zeros((CHUNK, CHUNK), F32)
        for hs, mask in zip(_HGRN_LEVELS, level_masks):
            if hs >= 4:
                x = -jnp.abs(b - _row_bcast(b, 2 * hs, hs - 1))
            elif hs == 2:
                nxt = pltpu.roll(log_f, CHUNK - 1, axis=0)
                prv = pltpu.roll(log_f, 1, axis=0)
                x = jnp.where(m4 == 0, nxt, jnp.where(m4 == 1, 0.0, jnp.where(m4 == 2, log_f, log_f + prv)))
            else:
                x = jnp.where((row & 1) == 1, log_f, 0.0)
            w = jnp.exp2(x)
            a_l = (q * w).astype(BF16)
            k_l = (key * w).astype(BF16)
            s_l = lax.dot_general(a_l, k_l, (((1,), (1,)), ((), ())), preferred_element_type=F32)
            scores = scores + jnp.where(mask, s_l, 0.0)
        s_d = lax.dot_general(q_bf, key_bf, (((1,), (1,)), ((), ())), preferred_element_type=F32)
        scores = scores + jnp.where(r64 == c64, s_d, 0.0)
        intra = jnp.dot(scores.astype(BF16), v_bf, preferred_element_type=F32)

        st = st_ref[...]
        qe = (q * jnp.exp2(b)).astype(BF16)
        inter = lax.dot_general(qe, st.astype(BF16), (((1,), (1,)), ((), ())), preferred_element_type=F32)
        kd = (key * jnp.exp2(b_last - b)).astype(BF16)
        upd = lax.dot_general(v_bf, kd, (((0,), (0,)), ((), ())), preferred_element_type=F32)
        st_ref[...] = st * jnp.exp2(b_last) + upd

        out = intra + inter
        y = out * lax.rsqrt(jnp.mean(out * out, axis=-1, keepdims=True) + EPS) * nw
        o_ref[0, sl, :] = (y * _silu(g)).astype(BF16)
        return carry

    lax.fori_loop(0, nc, chunk, 0, unroll=_HGRN_UNROLL)


def _hgrn(proj_hm, lb_l, norm_w_l):
    bsz, _, s_len, _ = proj_hm.shape

    def slab(off):
        return pl.BlockSpec((1, 1, s_len, HEAD_DIM), lambda b, h: (b, off + h, 0, 0))

    return pl.pallas_call(
        _hgrn_kernel,
        grid=(bsz, A_HEADS),
        in_specs=[
            slab(0), slab(A_HEADS), slab(2 * A_HEADS), slab(3 * A_HEADS),
            pl.BlockSpec((1, 1, HEAD_DIM), lambda b, h: (h, 0, 0)),
            pl.BlockSpec((1, HEAD_DIM), lambda b, h: (0, 0)),
        ],
        out_specs=pl.BlockSpec((1, s_len, HEAD_DIM), lambda b, h: (b, 0, h)),
        out_shape=jax.ShapeDtypeStruct((bsz, s_len, A_WIDTH), BF16),
        scratch_shapes=[pltpu.VMEM((HEAD_DIM, HEAD_DIM), F32)],
        compiler_params=_cparams(2),
        name="hgrn2",
    )(proj_hm, proj_hm, proj_hm, proj_hm, lb_l, norm_w_l)


def _head_rms(x, w):
    return x * lax.rsqrt(jnp.mean(x * x, axis=-1, keepdims=True) + EPS) * w


def _attn_kernel(q_ref, k_ref, v_ref, bias_ref, qw_ref, kw_ref, o_ref, kp_ref, vp_ref):
    s_len = q_ref.shape[2]
    qw = qw_ref[...]
    kw = kw_ref[...]
    scale = HEAD_DIM ** -0.5
    col = lax.broadcasted_iota(I32, (_ATTN_QROWS, _ATTN_KROWS), 1)

    kp_ref[0:BAND_PAD, :] = jnp.zeros((BAND_PAD, HEAD_DIM), BF16)
    vp_ref[0:BAND_PAD, :] = jnp.zeros((BAND_PAD, HEAD_DIM), BF16)
    blk = _ATTN_QROWS

    def prep(i, carry):
        src = pl.ds(pl.multiple_of(i * blk, blk), blk)
        dst = pl.ds(pl.multiple_of(BAND_PAD + i * blk, CHUNK), blk)
        kp_ref[dst, :] = _head_rms(k_ref[0, 0, src, :].astype(F32), kw).astype(BF16)
        vp_ref[dst, :] = v_ref[0, 0, src, :]
        return carry

    lax.fori_loop(0, s_len // blk, prep, 0, unroll=2 if (s_len // blk) % 2 == 0 else 1)

    def group(gi, carry):
        r0 = pl.multiple_of(gi * _ATTN_QROWS, _ATTN_QROWS)
        qn = _head_rms(q_ref[0, 0, pl.ds(r0, _ATTN_QROWS), :].astype(F32), qw).astype(BF16)
        kc = kp_ref[pl.ds(r0, _ATTN_KROWS), :]
        vc = vp_ref[pl.ds(r0, _ATTN_KROWS), :]
        s = lax.dot_general(qn, kc, (((1,), (1,)), ((), ())), preferred_element_type=F32) * scale + bias_ref[0]
        s = jnp.where(col + r0 >= BAND_PAD, s, MASK_VALUE)
        m = jnp.max(s, axis=-1, keepdims=True)
        p = jnp.exp(s - m)
        den = jnp.sum(p, axis=-1, keepdims=True)
        o = jnp.dot(p.astype(BF16), vc, preferred_element_type=F32) / den
        o_ref[0, pl.ds(r0, _ATTN_QROWS), :] = o.astype(BF16)
        return carry

    lax.fori_loop(0, s_len // _ATTN_QROWS, group, 0, unroll=_ATTN_UNROLL)


def _band_attention(proj_hm, bias_grp, qn_w_l, kn_w_l):
    bsz, _, s_len, _ = proj_hm.shape
    assert s_len % _ATTN_QROWS == 0
    base = 4 * A_HEADS

    def slab(off):
        return pl.BlockSpec((1, 1, s_len, HEAD_DIM), lambda h, b: (b, base + off + h, 0, 0))

    return pl.pallas_call(
        _attn_kernel,
        grid=(B_HEADS, bsz),
        in_specs=[
            slab(0), slab(B_HEADS), slab(2 * B_HEADS),
            pl.BlockSpec((1, _ATTN_QROWS, _ATTN_KROWS), lambda h, b: (h, 0, 0)),
            pl.BlockSpec((1, HEAD_DIM), lambda h, b: (0, 0)),
            pl.BlockSpec((1, HEAD_DIM), lambda h, b: (0, 0)),
        ],
        out_specs=pl.BlockSpec((1, s_len, HEAD_DIM), lambda h, b: (b, 0, h)),
        out_shape=jax.ShapeDtypeStruct((bsz, s_len, B_WIDTH), BF16),
        scratch_shapes=[
            pltpu.VMEM((BAND_PAD + s_len, HEAD_DIM), BF16),
            pltpu.VMEM((BAND_PAD + s_len, HEAD_DIM), BF16),
        ],
        compiler_params=_cparams(2),
        name="band_attention",
    )(proj_hm, proj_hm, proj_hm, bias_grp, qn_w_l, kn_w_l)


def _group_bias(rel_bias):
    diag = jnp.arange(-(CHUNK - 1), BAND)
    idx = jnp.clip(BAND_PAD - diag, -MAX_REL_DIST, MAX_REL_DIST) + MAX_REL_DIST
    per_diag = jnp.take(rel_bias.astype(F32), idx, axis=1)
    bias = jnp.stack([per_diag[:, CHUNK - 1 - c:CHUNK - 1 - c + BAND] for c in range(CHUNK)], axis=1)
    out = jnp.full((rel_bias.shape[0], _ATTN_QROWS, _ATTN_KROWS), MASK_VALUE, F32)
    for a in range(_ATTN_GROUP):
        out = out.at[:, a * CHUNK:(a + 1) * CHUNK, a * CHUNK:a * CHUNK + BAND].set(bias)
    return out


_POOL_HALO = 16


def _pool_kernel(p_ref, w_ref, sc_ref, o_ref, pad_ref, wbf_ref):
    s_len = p_ref.shape[2]
    gi = pl.program_id(1)
    blk = 256 if s_len % 256 == 0 else CHUNK
    nblk = s_len // blk
    hp = p_ref.shape[1]
    pad_ref[0:_POOL_HALO, :] = jnp.zeros((_POOL_HALO, pad_ref.shape[1]), F32)
    for hh in range(hp):
        pad_ref[_POOL_HALO:_POOL_HALO + s_len, hh * HEAD_DIM:(hh + 1) * HEAD_DIM] = p_ref[0, hh].astype(F32)
    wbf_ref[...] = w_ref[0].astype(BF16)
    scale = sc_ref[...]

    for g, win in enumerate(POOL_WINDOWS):
        @pl.when(gi == g)
        def _(win=win):
            def body(i, carry):
                r0 = pl.multiple_of(i * blk, blk)
                acc = pad_ref[pl.ds(r0, _POOL_HALO + blk), :]
                cur = acc[_POOL_HALO:, :]
                span = 1
                while span < win:
                    acc = acc + pltpu.roll(acc, span, axis=0)
                    span *= 2
                acc = acc[_POOL_HALO:, :]
                pos = lax.broadcasted_iota(I32, acc.shape, 0) + r0 + 1
                cnt = jnp.minimum(pos, win).astype(F32)
                mixed = (acc / cnt - cur).astype(BF16)
                out = jnp.dot(mixed, wbf_ref[...], preferred_element_type=F32) * scale
                o_ref[0, pl.ds(r0, blk), :] = out.astype(BF16)
                return carry

            lax.fori_loop(0, nblk, body, 0)


def _pool_mixer(proj_hm, w_pool, pool_scale3, layer):
    bsz, _, s_len, _ = proj_hm.shape
    hp = POOL_GROUP // HEAD_DIM
    base = (4 * A_HEADS + 3 * B_HEADS) // hp
    n_grp = len(POOL_WINDOWS)
    return pl.pallas_call(
        _pool_kernel,
        grid=(bsz, n_grp),
        in_specs=[
            pl.BlockSpec((1, hp, s_len, HEAD_DIM), lambda b, g: (b, base + g, 0, 0)),
            pl.BlockSpec((None, 1, POOL_GROUP, POOL_GROUP), lambda b, g: (layer, g, 0, 0)),
            pl.BlockSpec((None, 1, POOL_GROUP), lambda b, g: (layer, 0, g)),
        ],
        out_specs=pl.BlockSpec((1, s_len, POOL_GROUP), lambda b, g: (b, 0, g)),
        out_shape=jax.ShapeDtypeStruct((bsz, s_len, C_WIDTH), BF16),
        scratch_shapes=[
            pltpu.VMEM((_POOL_HALO + s_len, POOL_GROUP), F32),
            pltpu.VMEM((POOL_GROUP, POOL_GROUP), BF16),
        ],
        compiler_params=_cparams(2),
        name="pool_mixer",
    )(proj_hm, w_pool, pool_scale3)


def _gate_row(mod_ref, ada_ref, gate_idx):
    return mod_ref[0, gate_idx:gate_idx + 1, :] + ada_ref[gate_idx:gate_idx + 1, :]


def _outproj_kernel(ya_ref, yb_ref, yc_ref, w_ref, x_ref, mod_ref, ada_ref, o_ref, wbf_ref, *, gate_idx):
    @pl.when(pl.program_id(1) == 0)
    def _():
        _cast_weight(w_ref, wbf_ref)

    acc = jnp.dot(ya_ref[...], wbf_ref[0:A_WIDTH, :], preferred_element_type=F32)
    acc = acc + jnp.dot(yb_ref[...], wbf_ref[A_WIDTH:A_WIDTH + B_WIDTH, :], preferred_element_type=F32)
    acc = acc + jnp.dot(yc_ref[...], wbf_ref[A_WIDTH + B_WIDTH:, :], preferred_element_type=F32)
    o_ref[...] = x_ref[...] + _gate_row(mod_ref, ada_ref, gate_idx) * acc


def _out_proj(ya, yb, yc, w_o, x2, mod, ada_table, layer, s_len, gate_idx):
    t, d = x2.shape
    tm = min(TM, s_len)
    tn = TN
    per_b = s_len // tm
    return pl.pallas_call(
        functools.partial(_outproj_kernel, gate_idx=gate_idx),
        grid=(d // tn, t // tm),
        in_specs=[
            pl.BlockSpec((tm, A_WIDTH), lambda j, i: (i, 0)),
            pl.BlockSpec((tm, B_WIDTH), lambda j, i: (i, 0)),
            pl.BlockSpec((tm, C_WIDTH), lambda j, i: (i, 0)),
            pl.BlockSpec((None, d, tn), lambda j, i: (layer, 0, j)),
            pl.BlockSpec((tm, tn), lambda j, i: (i, j)),
            pl.BlockSpec((1, N_MOD, tn), lambda j, i: (i // per_b, 0, j)),
            pl.BlockSpec((None, N_MOD, tn), lambda j, i: (layer, 0, j)),
        ],
        out_specs=pl.BlockSpec((tm, tn), lambda j, i: (i, j)),
        out_shape=jax.ShapeDtypeStruct((t, d), F32),
        scratch_shapes=[pltpu.VMEM((d, tn), BF16)],
        compiler_params=_cparams(2),
        name="out_proj",
    )(ya, yb, yc, w_o, x2, mod, ada_table)


def _tile_flags(te_ref, n_tiles):
    i = pl.program_id(1)
    prev = te_ref[jnp.maximum(i - 1, 0)]
    first = jnp.logical_or(i == 0, te_ref[i] != prev)
    valid = i < te_ref[n_tiles]
    return first, valid


def _glu_kernel(te_ref, x_ref, wg_ref, wu_ref, o_ref, wg_bf, wu_bf, *, n_tiles, packed):
    first, valid = _tile_flags(te_ref, n_tiles)

    @pl.when(first)
    def _():
        _cast_weight(wg_ref, wg_bf)
        _cast_weight(wu_ref, wu_bf)

    @pl.when(valid)
    def _():
        if packed:
            half = x_ref.shape[1]
            lo, hi = _unpack_bf16_pairs(x_ref[...])
            g = jnp.dot(lo, wg_bf[0:half, :], preferred_element_type=F32)
            g = g + jnp.dot(hi, wg_bf[half:, :], preferred_element_type=F32)
            u = jnp.dot(lo, wu_bf[0:half, :], preferred_element_type=F32)
            u = u + jnp.dot(hi, wu_bf[half:, :], preferred_element_type=F32)
        else:
            x = x_ref[...]
            g = jnp.dot(x, wg_bf[...], preferred_element_type=F32)
            u = jnp.dot(x, wu_bf[...], preferred_element_type=F32)
        o_ref[...] = (_silu(g) * u).astype(BF16)

    @pl.when(jnp.logical_not(valid))
    def _():
        o_ref[...] = jnp.zeros_like(o_ref)


def _glu(x, wg, wu, te, widx, tm, tn, packed):
    rows, xc = x.shape
    d, f = wg.shape[-2], wg.shape[-1]
    n_tiles = rows // tm
    lead = len(wg.shape) - 3

    def w_map(j, i, te_ref):
        return tuple(widx[:lead]) + (te_ref[i], 0, j)

    w_block = (None,) * (lead + 1) + (d, tn)
    grid_spec = pltpu.PrefetchScalarGridSpec(
        num_scalar_prefetch=1,
        grid=(f // tn, n_tiles),
        in_specs=[
            pl.BlockSpec((tm, xc), lambda j, i, te_ref: (i, 0)),
            pl.BlockSpec(w_block, w_map),
            pl.BlockSpec(w_block, w_map),
        ],
        out_specs=pl.BlockSpec((tm, tn), lambda j, i, te_ref: (i, j)),
        scratch_shapes=[pltpu.VMEM((d, tn), BF16), pltpu.VMEM((d, tn), BF16)],
    )
    return pl.pallas_call(
        functools.partial(_glu_kernel, n_tiles=n_tiles, packed=packed),
        grid_spec=grid_spec,
        out_shape=jax.ShapeDtypeStruct((rows, f), BF16),
        compiler_params=_cparams(2),
        name="glu_pair",
    )(te, x, wg, wu)


def _down_res_kernel(a_ref, w_ref, r_ref, mod_ref, ada_ref, o_ref, wbf_ref, *, gate_idx):
    @pl.when(pl.program_id(1) == 0)
    def _():
        _cast_weight(w_ref, wbf_ref)

    acc = jnp.dot(a_ref[...], wbf_ref[...], preferred_element_type=F32)
    o_ref[...] = r_ref[...] + _gate_row(mod_ref, ada_ref, gate_idx) * acc


def _down_residual(a, w_down, widx, k_half, n_half, res, mod, ada_table, layer, s_len, gate_idx):
    t, d = res.shape
    kh = a.shape[1] // n_half
    tm = min(TM_E, s_len)
    tn = TN
    per_b = s_len // tm
    return pl.pallas_call(
        functools.partial(_down_res_kernel, gate_idx=gate_idx),
        grid=(d // tn, t // tm),
        in_specs=[
            pl.BlockSpec((tm, kh), lambda j, i: (i, k_half)),
            pl.BlockSpec((None, kh, tn), lambda j, i: (widx, k_half, j)),
            pl.BlockSpec((tm, tn), lambda j, i: (i, j)),
            pl.BlockSpec((1, N_MOD, tn), lambda j, i: (i // per_b, 0, j)),
            pl.BlockSpec((None, N_MOD, tn), lambda j, i: (layer, 0, j)),
        ],
        out_specs=pl.BlockSpec((tm, tn), lambda j, i: (i, j)),
        out_shape=jax.ShapeDtypeStruct((t, d), F32),
        scratch_shapes=[pltpu.VMEM((kh, tn), BF16)],
        compiler_params=_cparams(2),
        name="down_residual",
    )(a, w_down, res, mod, ada_table)


def _down_group_kernel(te_ref, a_ref, w_ref, o_ref, wbf_ref, *, n_tiles):
    first, valid = _tile_flags(te_ref, n_tiles)

    @pl.when(first)
    def _():
        _cast_weight(w_ref, wbf_ref)

    @pl.when(valid)
    def _():
        o_ref[...] = _pack_bf16_pairs(jnp.dot(a_ref[...], wbf_ref[...], preferred_element_type=F32))

    @pl.when(jnp.logical_not(valid))
    def _():
        o_ref[...] = jnp.zeros_like(o_ref)


def _down_grouped(a, w_down, moe_idx, te, tm):
    rows, k = a.shape
    d = w_down.shape[-1]
    tn = TN
    n_tiles = rows // tm
    grid_spec = pltpu.PrefetchScalarGridSpec(
        num_scalar_prefetch=1,
        grid=(d // tn, n_tiles),
        in_specs=[
            pl.BlockSpec((tm, k), lambda j, i, te_ref: (i, 0)),
            pl.BlockSpec((None, None, k, tn), lambda j, i, te_ref: (moe_idx, te_ref[i], 0, j)),
        ],
        out_specs=pl.BlockSpec((tm, tn // 2), lambda j, i, te_ref: (i, j)),
        scratch_shapes=[pltpu.VMEM((k, tn), BF16)],
    )
    return pl.pallas_call(
        functools.partial(_down_group_kernel, n_tiles=n_tiles),
        grid_spec=grid_spec,
        out_shape=jax.ShapeDtypeStruct((rows, d // 2), U32),
        compiler_params=_cparams(2),
        name="down_grouped",
    )(te, a, w_down)


def _route_kernel(lg_ref, tbl_ref, cnt_ref, carry_ref):
    i = pl.program_id(0)

    @pl.when(i == 0)
    def _():
        carry_ref[...] = jnp.zeros_like(carry_ref)

    tm = lg_ref.shape[0]
    lane = lax.broadcasted_iota(I32, (tm, LANES), 1).astype(F32)
    neg = jnp.float32(-jnp.inf)
    lg = jnp.where(lane < N_EXPERTS, lg_ref[...], neg)
    m1 = jnp.max(lg, axis=-1, keepdims=True)
    e1 = jnp.min(jnp.where(lg == m1, lane, float(LANES)), axis=-1, keepdims=True)
    lg2 = jnp.where(lane == e1, neg, lg)
    m2 = jnp.max(lg2, axis=-1, keepdims=True)
    e2 = jnp.min(jnp.where(lg2 == m2, lane, float(LANES)), axis=-1, keepdims=True)
    tt = jnp.exp(m2 - m1)
    w1 = 1.0 / (1.0 + tt)
    w2 = tt * w1

    memb = jnp.logical_or(lane == e1, lane == e2).astype(F32)
    rr = lax.broadcasted_iota(I32, (tm, tm), 0)
    cc = lax.broadcasted_iota(I32, (tm, tm), 1)
    strict_lower = (cc < rr).astype(BF16)
    pos = jnp.dot(strict_lower, memb.astype(BF16), preferred_element_type=F32) + carry_ref[...]
    p1 = jnp.sum(jnp.where(lane == e1, pos, 0.0), axis=-1, keepdims=True)
    p2 = jnp.sum(jnp.where(lane == e2, pos, 0.0), axis=-1, keepdims=True)
    carry_ref[...] = carry_ref[...] + jnp.sum(memb, axis=0, keepdims=True)

    out = jnp.where(lane == 0, e1, 0.0)
    out = jnp.where(lane == 1, e2, out)
    out = jnp.where(lane == 2, p1, out)
    out = jnp.where(lane == 3, p2, out)
    out = jnp.where(lane == 4, w1, out)
    out = jnp.where(lane == 5, w2, out)
    tbl_ref[...] = out
    cnt_ref[...] = carry_ref[...]


def _route(logits):
    t = logits.shape[0]
    tm = TM_ROUTE
    return pl.pallas_call(
        _route_kernel,
        grid=(t // tm,),
        in_specs=[pl.BlockSpec((tm, LANES), lambda i: (i, 0))],
        out_specs=[
            pl.BlockSpec((tm, LANES), lambda i: (i, 0)),
            pl.BlockSpec((1, LANES), lambda i: (0, 0)),
        ],
        out_shape=[
            jax.ShapeDtypeStruct((t, LANES), F32),
            jax.ShapeDtypeStruct((1, LANES), F32),
        ],
        scratch_shapes=[pltpu.VMEM((1, LANES), F32)],
        compiler_params=_cparams(1),
        name="moe_route",
    )(logits)


def _dispatch_kernel(dest_ref, hp_ref, xs_in_ref, xs_ref, sem):
    del xs_in_ref
    ch = hp_ref.shape[0]

    def issue(r, carry):
        src = hp_ref.at[pl.ds(r, 1)]
        pltpu.make_async_copy(src, xs_ref.at[pl.ds(dest_ref[2 * r], 1)], sem).start()
        pltpu.make_async_copy(src, xs_ref.at[pl.ds(dest_ref[2 * r + 1], 1)], sem).start()
        return carry

    lax.fori_loop(0, ch, issue, 0, unroll=8)

    for _ in range(2):
        pltpu.make_async_copy(hp_ref, xs_ref.at[pl.ds(0, ch)], sem).wait()


def _dispatch(hp, dest_flat, rows):
    t, w = hp.shape
    ch = CH_DISP
    xs0 = jnp.zeros((rows, w), hp.dtype)
    return pl.pallas_call(
        _dispatch_kernel,
        grid=(t // ch,),
        in_specs=[
            pl.BlockSpec((2 * ch,), lambda i: (i,), memory_space=pltpu.SMEM),
            pl.BlockSpec((ch, w), lambda i: (i, 0)),
            pl.BlockSpec(memory_space=pl.ANY),
        ],
        out_specs=pl.BlockSpec(memory_space=pl.ANY),
        out_shape=jax.ShapeDtypeStruct((rows, w), hp.dtype),
        scratch_shapes=[pltpu.SemaphoreType.DMA(())],
        input_output_aliases={2: 0},
        compiler_params=_cparams(1),
        name="moe_dispatch",
    )(dest_flat, hp, xs0)


def _combine_kernel(dcur_ref, dnext_ref, ys_ref, x_ref, tbl_ref, mod_ref, ada_ref, o_ref, buf, sem, *, gate_idx):
    i = pl.program_id(0)
    n = pl.num_programs(0)
    tm = x_ref.shape[0]
    slot = i % 2

    def issue(d_ref, sl):
        def body(r, carry):
            for k in range(2):
                pltpu.make_async_copy(
                    ys_ref.at[pl.ds(d_ref[2 * r + k], 1)], buf.at[sl, k, pl.ds(r, 1)], sem.at[sl]).start()
            return carry

        lax.fori_loop(0, tm, body, 0, unroll=8)

    @pl.when(i == 0)
    def _():
        issue(dcur_ref, 0)

    @pl.when(i + 1 < n)
    def _():
        issue(dnext_ref, 1 - slot)

    for k in range(2):
        pltpu.make_async_copy(ys_ref.at[pl.ds(0, tm)], buf.at[slot, k], sem.at[slot]).wait()

    gate = _gate_row(mod_ref, ada_ref, gate_idx)
    hi_mask = jnp.uint32(0xFFFF0000)
    half = TN // 2
    rb = 32

    def rows(r, carry):
        sl = pl.ds(pl.multiple_of(r * rb, rb), rb)
        w1 = tbl_ref[sl, 4:5]
        w2 = tbl_ref[sl, 5:6]
        for jb in range(o_ref.shape[1] // TN):
            y1 = buf[slot, 0, sl, jb * half:(jb + 1) * half]
            y2 = buf[slot, 1, sl, jb * half:(jb + 1) * half]
            f_lo = w1 * lax.bitcast_convert_type(y1 << 16, F32) + w2 * lax.bitcast_convert_type(y2 << 16, F32)
            f_hi = (w1 * lax.bitcast_convert_type(y1 & hi_mask, F32)
                    + w2 * lax.bitcast_convert_type(y2 & hi_mask, F32))
            for part, f in ((0, f_lo), (1, f_hi)):
                cols = slice(jb * TN + part * half, jb * TN + (part + 1) * half)
                o_ref[sl, cols] = x_ref[sl, cols] + gate[:, cols] * f
        return carry

    lax.fori_loop(0, tm // rb, rows, 0)


def _combine(ys, dest_flat, x2, tbl, mod, ada_table, layer, s_len, gate_idx):
    t, d = x2.shape
    tm = TM_COMB
    per_b = s_len // tm
    n = t // tm
    return pl.pallas_call(
        functools.partial(_combine_kernel, gate_idx=gate_idx),
        grid=(n,),
        in_specs=[
            pl.BlockSpec((2 * tm,), lambda i: (i,), memory_space=pltpu.SMEM),
            pl.BlockSpec((2 * tm,), lambda i: (jnp.minimum(i + 1, n - 1),), memory_space=pltpu.SMEM),
            pl.BlockSpec(memory_space=pl.ANY),
            pl.BlockSpec((tm, d), lambda i: (i, 0)),
            pl.BlockSpec((tm, LANES), lambda i: (i, 0)),
            pl.BlockSpec((1, N_MOD, d), lambda i: (i // per_b, 0, 0)),
            pl.BlockSpec((None, N_MOD, d), lambda i: (layer, 0, 0)),
        ],
        out_specs=pl.BlockSpec((tm, d), lambda i: (i, 0)),
        out_shape=jax.ShapeDtypeStruct((t, d), F32),
        scratch_shapes=[
            pltpu.VMEM((2, 2, tm, d // 2), U32),
            pltpu.SemaphoreType.DMA((2,)),
        ],
        compiler_params=_cparams(1),
        name="moe_combine",
    )(dest_flat, dest_flat, ys, x2, tbl, mod, ada_table)


def _moe_plan(tbl, cnt, tm, n_tiles):
    e = tbl[:, 0:2].astype(I32)
    pos = tbl[:, 2:4].astype(I32)
    counts = cnt[0, :N_EXPERTS].astype(I32)
    padded = ((counts + tm - 1) // tm) * tm
    ends = jnp.cumsum(padded)
    offs = ends - padded
    expert_ids = jnp.arange(N_EXPERTS, dtype=I32)
    dest = jnp.sum(jnp.where(e[..., None] == expert_ids, offs, 0), axis=-1) + pos
    tile_start = jnp.arange(n_tiles, dtype=I32) * tm
    n_used = ends[-1] // tm
    te = jnp.sum((tile_start[:, None] >= ends[None, :]).astype(I32), axis=1)
    last = jnp.minimum(jnp.maximum(n_used - 1, 0), n_tiles - 1)
    te = jnp.where(jnp.arange(n_tiles) < n_used, jnp.minimum(te, N_EXPERTS - 1), te[last])
    te = jnp.concatenate([te.astype(I32), n_used.astype(I32)[None]])
    return dest.reshape(-1), te


def kernel(x, c, w_ada, b_ada, ada_table, norm_mix_w, w_in, lb_logits, hgrn_norm_w, q_norm_w, k_norm_w,
           rel_bias, w_pool, pool_scale, w_o, norm_ffn_w, ffn_w_gate, ffn_w_up, ffn_w_down, moe_w_router,
           moe_b_router, moe_w_gate, moe_w_up, moe_w_down):
    bsz, s_len, d = x.shape
    depth = w_in.shape[0]
    t = bsz * s_len
    x2 = x.reshape(t, d)

    mod = _adaln_mod(c, w_ada, b_ada).reshape(bsz, N_MOD, d)

    lb_p = jax.nn.softmax(lb_logits.astype(F32), axis=0)
    lb_all = (jnp.cumsum(lb_p, axis=0) - lb_p[0:1]).reshape(depth, A_HEADS, 1, HEAD_DIM)
    bias = _group_bias(rel_bias)
    pool_scale3 = pool_scale.reshape(depth, 1, C_WIDTH)
    n_moe = moe_w_router.shape[0]
    w_router_pad = jnp.pad(moe_w_router, ((0, 0), (0, 0), (0, LANES - N_EXPERTS)))
    b_router_pad = jnp.pad(moe_b_router, ((0, 0), (0, LANES - N_EXPERTS))).reshape(n_moe, 1, LANES)
    tm_dense = min(TM, s_len)
    dense_te = jnp.concatenate([jnp.zeros((t // tm_dense,), I32), jnp.full((1,), t // tm_dense, I32)])

    for l in range(depth):
        h = _norm(x2, mod, ada_table, norm_mix_w, l, s_len, 0)
        proj = _in_proj(h, w_in, l, bsz, s_len)
        y_a = _hgrn(proj, lb_all[l], hgrn_norm_w[l].reshape(1, HEAD_DIM))
        y_b = _band_attention(proj, bias, q_norm_w[l].reshape(1, HEAD_DIM), k_norm_w[l].reshape(1, HEAD_DIM))
        y_c = _pool_mixer(proj, w_pool, pool_scale3, l)
        x2 = _out_proj(y_a.reshape(t, A_WIDTH), y_b.reshape(t, B_WIDTH), y_c.reshape(t, C_WIDTH),
                       w_o, x2, mod, ada_table, l, s_len, 2)
        if l % 2 == 0:
            j = l // 2
            h = _norm(x2, mod, ada_table, norm_ffn_w, l, s_len, 3)
            a = _glu(h, ffn_w_gate, ffn_w_up, dense_te, (), tm_dense, TN_GLU, packed=False)
            x2 = _down_residual(a, ffn_w_down, j, 0, 2, x2, mod, ada_table, l, s_len, 5)
            x2 = _down_residual(a, ffn_w_down, j, 1, 2, x2, mod, ada_table, l, s_len, 5)
        else:
            j = l // 2
            hp, logits = _norm_router(x2, mod, ada_table, norm_ffn_w, w_router_pad, b_router_pad, l, j, s_len, 3)
            tbl, cnt = _route(logits)
            rows = 2 * t + N_EXPERTS * TM_E
            n_tiles = rows // TM_E
            dest, te = _moe_plan(tbl, cnt, TM_E, n_tiles)
            xs = _dispatch(hp, dest, rows)
            a = _glu(xs, moe_w_gate, moe_w_up, te, (j,), TM_E, TN_GLU_E, packed=True)
            ys = _down_grouped(a, moe_w_down, j, te, TM_E)
            x2 = _combine(ys, dest, x2, tbl, mod, ada_table, l, s_len, 5)
    return x2.reshape(bsz, s_len, d)
```

```python
import functools

import jax
import jax.numpy as jnp
from jax import lax
from jax.experimental import pallas as pl
from jax.experimental.pallas import tpu as pltpu

F32 = jnp.float32
BF16 = jnp.bfloat16
U32 = jnp.uint32
I32 = jnp.int32

D_MODEL = 4096
CHUNK = 64
HEAD_DIM = 128
A_HEADS = 12
B_HEADS = 12
A_WIDTH = A_HEADS * HEAD_DIM
B_WIDTH = B_HEADS * HEAD_DIM
C_WIDTH = D_MODEL - A_WIDTH - B_WIDTH
POOL_WINDOWS = (2, 4, 8, 16)
POOL_GROUP = C_WIDTH // len(POOL_WINDOWS)
IN_COLS = 4 * A_WIDTH + 3 * B_WIDTH + C_WIDTH
IN_HEADS = IN_COLS // HEAD_DIM
BAND_CHUNKS = 9
BAND = BAND_CHUNKS * CHUNK
BAND_PAD = (BAND_CHUNKS - 1) * CHUNK
MAX_REL_DIST = 128
N_EXPERTS = 8
N_MOD = 6
EPS = 1e-6
F_MIN = 1e-6
MASK_VALUE = -1e9

LANES = 128
SUBLANES = 8
VMEM_LIMIT_MB = 56

TM = 1024
TN = 512
TN_GLU = 256
TN_GLU_E = 512
TM_E = 512
TM_NORM = 512
TM_ROUTE = 512
TM_COMB = 256
CH_DISP = 256


def _cparams(n_grid, vmem_mb=VMEM_LIMIT_MB):
    return pltpu.CompilerParams(
        dimension_semantics=("arbitrary",) * n_grid,
        vmem_limit_bytes=vmem_mb * 1024 * 1024,
    )


def _silu(x):
    return x * jax.nn.sigmoid(x)


def _cast_weight(w_ref, wbf_ref):
    k = w_ref.shape[0]
    rows = 512 if k % 512 == 0 else 128
    assert k % rows == 0

    def body(r, carry):
        sl = pl.ds(pl.multiple_of(r * rows, rows), rows)
        wbf_ref[sl, :] = w_ref[sl, :].astype(BF16)
        return carry

    lax.fori_loop(0, k // rows, body, 0)


def _mod_kernel(c_ref, w_ref, b_ref, o_ref):
    s = _silu(c_ref[...]).astype(BF16)
    w = w_ref[...].astype(BF16)
    o_ref[...] = jnp.dot(s, w, preferred_element_type=F32) + b_ref[...]


def _adaln_mod(c, w_ada, b_ada):
    bsz, d = c.shape
    n = w_ada.shape[1]
    tn = 1024
    return pl.pallas_call(
        _mod_kernel,
        grid=(n // tn,),
        in_specs=[
            pl.BlockSpec((bsz, d), lambda j: (0, 0)),
            pl.BlockSpec((d, tn), lambda j: (0, j)),
            pl.BlockSpec((1, tn), lambda j: (0, j)),
        ],
        out_specs=pl.BlockSpec((bsz, tn), lambda j: (0, j)),
        out_shape=jax.ShapeDtypeStruct((bsz, n), F32),
        compiler_params=_cparams(1),
        name="adaln_mod",
    )(c, w_ada, b_ada.reshape(1, n))


def _modulated_norm(x_ref, mod_ref, ada_ref, w_ref, shift_idx):
    x = x_ref[...]
    y = x * lax.rsqrt(jnp.mean(x * x, axis=-1, keepdims=True) + EPS) * w_ref[...]
    m = mod_ref[0] + ada_ref[...]
    shift = m[shift_idx:shift_idx + 1]
    scale = m[shift_idx + 1:shift_idx + 2]
    return y * (1.0 + scale) + shift


def _norm_kernel(x_ref, mod_ref, ada_ref, w_ref, h_ref, *, shift_idx):
    h_ref[...] = _modulated_norm(x_ref, mod_ref, ada_ref, w_ref, shift_idx).astype(BF16)


def _pack_bf16_pairs(h):
    n = h.shape[1] // 2
    lo = lax.bitcast_convert_type(h[:, :n].astype(BF16).astype(F32), U32)
    hi = lax.bitcast_convert_type(h[:, n:].astype(BF16).astype(F32), U32)
    return (hi & jnp.uint32(0xFFFF0000)) | (lo >> 16)


def _unpack_bf16_pairs(u):
    lo = lax.bitcast_convert_type(u << 16, F32).astype(BF16)
    hi = lax.bitcast_convert_type(u & jnp.uint32(0xFFFF0000), F32).astype(BF16)
    return lo, hi


def _norm_router_kernel(x_ref, mod_ref, ada_ref, w_ref, wr_ref, br_ref, hp_ref, lg_ref, *, shift_idx):
    h = _modulated_norm(x_ref, mod_ref, ada_ref, w_ref, shift_idx)
    hp_ref[...] = _pack_bf16_pairs(h)
    wr = wr_ref[...]
    h_hi = h.astype(BF16)
    h_lo = (h - h_hi.astype(F32)).astype(BF16)
    w_hi = wr.astype(BF16)
    w_lo = (wr - w_hi.astype(F32)).astype(BF16)
    lg = jnp.dot(h_hi, w_hi, preferred_element_type=F32)
    lg = lg + jnp.dot(h_hi, w_lo, preferred_element_type=F32)
    lg = lg + jnp.dot(h_lo, w_hi, preferred_element_type=F32)
    lg_ref[...] = lg + br_ref[...]


def _norm_specs(t, d, s_len, tm, layer):
    per_b = s_len // tm
    return [
        pl.BlockSpec((tm, d), lambda i: (i, 0)),
        pl.BlockSpec((1, N_MOD, d), lambda i: (i // per_b, 0, 0)),
        pl.BlockSpec((None, N_MOD, d), lambda i: (layer, 0, 0)),
        pl.BlockSpec((None, 1, d), lambda i: (layer, 0, 0)),
    ]


def _norm(x2, mod, ada_table, norm_w, layer, s_len, shift_idx):
    t, d = x2.shape
    tm = min(TM_NORM, s_len)
    return pl.pallas_call(
        functools.partial(_norm_kernel, shift_idx=shift_idx),
        grid=(t // tm,),
        in_specs=_norm_specs(t, d, s_len, tm, layer),
        out_specs=pl.BlockSpec((tm, d), lambda i: (i, 0)),
        out_shape=jax.ShapeDtypeStruct((t, d), BF16),
        compiler_params=_cparams(1),
        name="norm_mod",
    )(x2, mod, ada_table, norm_w.reshape(norm_w.shape[0], 1, d))


def _norm_router(x2, mod, ada_table, norm_w, w_router_pad, b_router_pad, layer, moe_idx, s_len, shift_idx):
    t, d = x2.shape
    tm = min(TM_NORM, s_len)
    return pl.pallas_call(
        functools.partial(_norm_router_kernel, shift_idx=shift_idx),
        grid=(t // tm,),
        in_specs=_norm_specs(t, d, s_len, tm, layer) + [
            pl.BlockSpec((None, d, LANES), lambda i: (moe_idx, 0, 0)),
            pl.BlockSpec((None, 1, LANES), lambda i: (moe_idx, 0, 0)),
        ],
        out_specs=[
            pl.BlockSpec((tm, d // 2), lambda i: (i, 0)),
            pl.BlockSpec((tm, LANES), lambda i: (i, 0)),
        ],
        out_shape=[
            jax.ShapeDtypeStruct((t, d // 2), U32),
            jax.ShapeDtypeStruct((t, LANES), F32),
        ],
        compiler_params=_cparams(1),
        name="norm_mod_router",
    )(x2, mod, ada_table, norm_w.reshape(norm_w.shape[0], 1, d), w_router_pad, b_router_pad)


def _inproj_kernel(h_ref, w_ref, o_ref, wbf_ref):
    @pl.when(pl.program_id(1) == 0)
    def _():
        _cast_weight(w_ref, wbf_ref)

    acc = jnp.dot(h_ref[...], wbf_ref[...], preferred_element_type=F32)
    for hh in range(o_ref.shape[1]):
        o_ref[0, hh] = acc[:, hh * HEAD_DIM:(hh + 1) * HEAD_DIM].astype(BF16)


def _in_proj(h, w_in, layer, bsz, s_len):
    t, d = h.shape
    n = w_in.shape[2]
    tm = min(TM, s_len)
    tn = TN
    per_b = s_len // tm
    hpt = tn // HEAD_DIM
    return pl.pallas_call(
        _inproj_kernel,
        grid=(n // tn, t // tm),
        in_specs=[
            pl.BlockSpec((tm, d), lambda j, i: (i, 0)),
            pl.BlockSpec((None, d, tn), lambda j, i: (layer, 0, j)),
        ],
        out_specs=pl.BlockSpec((1, hpt, tm, HEAD_DIM), lambda j, i: (i // per_b, j, i % per_b, 0)),
        out_shape=jax.ShapeDtypeStruct((bsz, n // HEAD_DIM, s_len, HEAD_DIM), BF16),
        scratch_shapes=[pltpu.VMEM((d, tn), BF16)],
        compiler_params=_cparams(2),
        name="in_proj",
    )(h, w_in)


_HGRN_LEVELS = (32, 16, 8, 4, 2, 1)
_ATTN_GROUP = 4
_ATTN_QROWS = _ATTN_GROUP * CHUNK
_ATTN_KROWS = BAND_PAD + _ATTN_QROWS


def _row_bcast(x, period, row):
    n, c = x.shape
    x3 = x.reshape(n // period, period, c)
    return jnp.broadcast_to(x3[:, row:row + 1, :], x3.shape).reshape(n, c)


def _hgrn_chunk_fn(q_ref, z_ref, v_ref, g_ref, lb_ref, nw_ref, o_ref, st_ref):
    lb = lb_ref[0]
    nw = nw_ref[...]
    row = lax.broadcasted_iota(I32, (CHUNK, HEAD_DIM), 0)
    sub3 = lax.broadcasted_iota(I32, (CHUNK // SUBLANES, SUBLANES, HEAD_DIM), 1)
    r64 = lax.broadcasted_iota(I32, (CHUNK, CHUNK), 0)
    c64 = lax.broadcasted_iota(I32, (CHUNK, CHUNK), 1)
    level_masks = []
    for hs in _HGRN_LEVELS:
        pair = 2 * hs
        same = (r64 & ~(pair - 1)) == (c64 & ~(pair - 1))
        level_masks.append(same & ((r64 & (pair - 1)) >= hs) & ((c64 & (pair - 1)) < hs))

    st_ref[...] = jnp.zeros_like(st_ref)

    def chunk(ci, carry):
        sl = pl.ds(pl.multiple_of(ci * CHUNK, CHUNK), CHUNK)
        q_bf = q_ref[0, 0, sl, :]
        q = q_bf.astype(F32)
        z = z_ref[0, 0, sl, :].astype(F32)
        v_bf = v_ref[0, 0, sl, :]
        g = g_ref[0, 0, sl, :].astype(F32)

        e = jnp.exp(-jnp.abs(z))
        r = 1.0 / (1.0 + e)
        sig_pos = jnp.where(z >= 0, r, e * r)
        sig_neg = jnp.where(z >= 0, e * r, r)
        f_gate = lb + (1.0 - lb) * sig_pos
        log_f = jnp.log2(jnp.maximum(f_gate, F_MIN))
        key = (1.0 - lb) * sig_neg

        grp = CHUNK // SUBLANES
        b3 = log_f.reshape(grp, SUBLANES, HEAD_DIM)
        for sh in (1, 2, 4):
            b3 = b3 + jnp.where(sub3 >= sh, pltpu.roll(b3, sh, axis=1), 0.0)
        run = jnp.zeros((1, 1, HEAD_DIM), F32)
        parts = []
        for gidx in range(grp):
            parts.append(b3[gidx:gidx + 1] + run)
            run = run + b3[gidx:gidx + 1, SUBLANES - 1:SUBLANES, :]
        b = jnp.concatenate(parts, axis=0).reshape(CHUNK, HEAD_DIM)
        b_last = run.reshape(1, HEAD_DIM)

        m4 = row & 3
        key_bf = key.astype(BF16)
        scores = jnp.zeros((CHUNK, CHUNK), F32)
        for hs, mask in zip(_HGRN_LEVELS, level_masks):
            if hs >= 4:
                x = -jnp.abs(b - _row_bcast(b, 2 * hs, hs - 1))
            elif hs == 2:
                nxt = pltpu.roll(log_f, CHUNK - 1, axis=0)
                prv = pltpu.roll(log_f, 1, axis=0)
                x = jnp.where(m4 == 0, nxt, jnp.where(m4 == 1, 0.0, jnp.where(m4 == 2, log_f, log_f + prv)))
            else:
                x = jnp.where((row & 1) == 1, log_f, 0.0)
            w = jnp.exp2(x)
            a_l = (q * w).astype(BF16)
            k_l = (key * w).astype(BF16)
            s_l = lax.dot_general(a_l, k_l, (((1,), (1,)), ((), ())), preferred_element_type=F32)
            scores = scores + jnp.where(mask, s_l, 0.0)
        s_d = lax.dot_general(q_bf, key_bf, (((1,), (1,)), ((), ())), preferred_element_type=F32)
        scores = scores + jnp.where(r64 == c64, s_d, 0.0)
        intra = jnp.dot(scores.astype(BF16), v_bf, preferred_element_type=F32)

        st = st_ref[...]
        qe = (q * jnp.exp2(b)).astype(BF16)
        inter = lax.dot_general(qe, st.astype(BF16), (((1,), (1,)), ((), ())), preferred_element_type=F32)
        kd = (key * jnp.exp2(b_last - b)).astype(BF16)
        upd = lax.dot_general(v_bf, kd, (((0,), (0,)), ((), ())), preferred_element_type=F32)
        st_ref[...] = st * jnp.exp2(b_last) + upd

        out = intra + inter
        y = out * lax.rsqrt(jnp.mean(out * out, axis=-1, keepdims=True) + EPS) * nw
        o_ref[0, sl, :] = (y * _silu(g)).astype(BF16)
        return carry

    return chunk


def _head_rms(x, w):
    return x * lax.rsqrt(jnp.mean(x * x, axis=-1, keepdims=True) + EPS) * w


def _attn_group_fn(q_ref, k_ref, v_ref, bias_ref, qw_ref, kw_ref, o_ref, kp_ref, vp_ref):
    s_len = q_ref.shape[2]
    qw = qw_ref[...]
    kw = kw_ref[...]
    scale = HEAD_DIM ** -0.5
    col = lax.broadcasted_iota(I32, (_ATTN_QROWS, _ATTN_KROWS), 1)

    kp_ref[0:BAND_PAD, :] = jnp.zeros((BAND_PAD, HEAD_DIM), BF16)
    vp_ref[0:BAND_PAD, :] = jnp.zeros((BAND_PAD, HEAD_DIM), BF16)
    blk = _ATTN_QROWS

    def prep(i, carry):
        src = pl.ds(pl.multiple_of(i * blk, blk), blk)
        dst = pl.ds(pl.multiple_of(BAND_PAD + i * blk, CHUNK), blk)
        kp_ref[dst, :] = _head_rms(k_ref[0, 0, src, :].astype(F32), kw).astype(BF16)
        vp_ref[dst, :] = v_ref[0, 0, src, :]
        return carry

    lax.fori_loop(0, s_len // blk, prep, 0, unroll=2 if (s_len // blk) % 2 == 0 else 1)

    def group(gi, carry):
        r0 = pl.multiple_of(gi * _ATTN_QROWS, _ATTN_QROWS)
        qn = _head_rms(q_ref[0, 0, pl.ds(r0, _ATTN_QROWS), :].astype(F32), qw).astype(BF16)
        kc = kp_ref[pl.ds(r0, _ATTN_KROWS), :]
        vc = vp_ref[pl.ds(r0, _ATTN_KROWS), :]
        s = lax.dot_general(qn, kc, (((1,), (1,)), ((), ())), preferred_element_type=F32) * scale + bias_ref[0]
        s = jnp.where(col + r0 >= BAND_PAD, s, MASK_VALUE)
        m = jnp.max(s, axis=-1, keepdims=True)
        p = jnp.exp(s - m)
        den = jnp.sum(p, axis=-1, keepdims=True)
        o = jnp.dot(p.astype(BF16), vc, preferred_element_type=F32) / den
        o_ref[0, pl.ds(r0, _ATTN_QROWS), :] = o.astype(BF16)
        return carry

    return group


def _mixers_kernel(qa_ref, za_ref, va_ref, ga_ref, lb_ref, nw_ref, qb_ref, kb_ref, vb_ref, bias_ref, qw_ref, kw_ref,
                   oa_ref, ob_ref, st_ref, kp_ref, vp_ref):
    s_len = qa_ref.shape[2]
    chunk = _hgrn_chunk_fn(qa_ref, za_ref, va_ref, ga_ref, lb_ref, nw_ref, oa_ref, st_ref)
    group = _attn_group_fn(qb_ref, kb_ref, vb_ref, bias_ref, qw_ref, kw_ref, ob_ref, kp_ref, vp_ref)

    def body(gi, carry):
        for u in range(_ATTN_GROUP):
            chunk(gi * _ATTN_GROUP + u, carry)
        group(gi, carry)
        return carry

    lax.fori_loop(0, s_len // _ATTN_QROWS, body, 0, unroll=2)


def _hgrn_attention(proj_hm, lb_l, norm_w_l, bias_grp, qn_w_l, kn_w_l):
    bsz, _, s_len, _ = proj_hm.shape
    assert s_len % _ATTN_QROWS == 0 and A_HEADS == B_HEADS

    def slab(off):
        return pl.BlockSpec((1, 1, s_len, HEAD_DIM), lambda h, b: (b, off + h, 0, 0))

    vec = pl.BlockSpec((1, HEAD_DIM), lambda h, b: (0, 0))
    out = pl.BlockSpec((1, s_len, HEAD_DIM), lambda h, b: (b, 0, h))
    base = 4 * A_HEADS
    return pl.pallas_call(
        _mixers_kernel,
        grid=(A_HEADS, bsz),
        in_specs=[
            slab(0), slab(A_HEADS), slab(2 * A_HEADS), slab(3 * A_HEADS),
            pl.BlockSpec((1, 1, HEAD_DIM), lambda h, b: (h, 0, 0)),
            vec,
            slab(base), slab(base + B_HEADS), slab(base + 2 * B_HEADS),
            pl.BlockSpec((1, _ATTN_QROWS, _ATTN_KROWS), lambda h, b: (h, 0, 0)),
            vec, vec,
        ],
        out_specs=[out, out],
        out_shape=[
            jax.ShapeDtypeStruct((bsz, s_len, A_WIDTH), BF16),
            jax.ShapeDtypeStruct((bsz, s_len, B_WIDTH), BF16),
        ],
        scratch_shapes=[
            pltpu.VMEM((HEAD_DIM, HEAD_DIM), F32),
            pltpu.VMEM((BAND_PAD + s_len, HEAD_DIM), BF16),
            pltpu.VMEM((BAND_PAD + s_len, HEAD_DIM), BF16),
        ],
        compiler_params=_cparams(2),
        name="hgrn_attention",
    )(proj_hm, proj_hm, proj_hm, proj_hm, lb_l, norm_w_l, proj_hm, proj_hm, proj_hm, bias_grp, qn_w_l, kn_w_l)


def _group_bias(rel_bias):
    diag = jnp.arange(-(CHUNK - 1), BAND)
    idx = jnp.clip(BAND_PAD - diag, -MAX_REL_DIST, MAX_REL_DIST) + MAX_REL_DIST
    per_diag = jnp.take(rel_bias.astype(F32), idx, axis=1)
    bias = jnp.stack([per_diag[:, CHUNK - 1 - c:CHUNK - 1 - c + BAND] for c in range(CHUNK)], axis=1)
    out = jnp.full((rel_bias.shape[0], _ATTN_QROWS, _ATTN_KROWS), MASK_VALUE, F32)
    for a in range(_ATTN_GROUP):
        out = out.at[:, a * CHUNK:(a + 1) * CHUNK, a * CHUNK:a * CHUNK + BAND].set(bias)
    return out


_POOL_HALO = 16


def _pool_kernel(p_ref, w_ref, sc_ref, o_ref, pad_ref, wbf_ref):
    s_len = p_ref.shape[2]
    gi = pl.program_id(1)
    blk = 256 if s_len % 256 == 0 else CHUNK
    nblk = s_len // blk
    hp = p_ref.shape[1]
    pad_ref[0:_POOL_HALO, :] = jnp.zeros((_POOL_HALO, pad_ref.shape[1]), F32)
    for hh in range(hp):
        pad_ref[_POOL_HALO:_POOL_HALO + s_len, hh * HEAD_DIM:(hh + 1) * HEAD_DIM] = p_ref[0, hh].astype(F32)
    wbf_ref[...] = w_ref[0].astype(BF16)
    scale = sc_ref[...]

    for g, win in enumerate(POOL_WINDOWS):
        @pl.when(gi == g)
        def _(win=win):
            def body(i, carry):
                r0 = pl.multiple_of(i * blk, blk)
                acc = pad_ref[pl.ds(r0, _POOL_HALO + blk), :]
                cur = acc[_POOL_HALO:, :]
                span = 1
                while span < win:
                    acc = acc + pltpu.roll(acc, span, axis=0)
                    span *= 2
                acc = acc[_POOL_HALO:, :]
                pos = lax.broadcasted_iota(I32, acc.shape, 0) + r0 + 1
                cnt = jnp.minimum(pos, win).astype(F32)
                mixed = (acc / cnt - cur).astype(BF16)
                out = jnp.dot(mixed, wbf_ref[...], preferred_element_type=F32) * scale
                o_ref[0, pl.ds(r0, blk), :] = out.astype(BF16)
                return carry

            lax.fori_loop(0, nblk, body, 0)


def _pool_mixer(proj_hm, w_pool, pool_scale3, layer):
    bsz, _, s_len, _ = proj_hm.shape
    hp = POOL_GROUP // HEAD_DIM
    base = (4 * A_HEADS + 3 * B_HEADS) // hp
    n_grp = len(POOL_WINDOWS)
    return pl.pallas_call(
        _pool_kernel,
        grid=(bsz, n_grp),
        in_specs=[
            pl.BlockSpec((1, hp, s_len, HEAD_DIM), lambda b, g: (b, base + g, 0, 0)),
            pl.BlockSpec((None, 1, POOL_GROUP, POOL_GROUP), lambda b, g: (layer, g, 0, 0)),
            pl.BlockSpec((None, 1, POOL_GROUP), lambda b, g: (layer, 0, g)),
        ],
        out_specs=pl.BlockSpec((1, s_len, POOL_GROUP), lambda b, g: (b, 0, g)),
        out_shape=jax.ShapeDtypeStruct((bsz, s_len, C_WIDTH), BF16),
        scratch_shapes=[
            pltpu.VMEM((_POOL_HALO + s_len, POOL_GROUP), F32),
            pltpu.VMEM((POOL_GROUP, POOL_GROUP), BF16),
        ],
        compiler_params=_cparams(2),
        name="pool_mixer",
    )(proj_hm, w_pool, pool_scale3)


def _gate_row(mod_ref, ada_ref, gate_idx):
    return mod_ref[0, gate_idx:gate_idx + 1, :] + ada_ref[gate_idx:gate_idx + 1, :]


def _outproj_kernel(ya_ref, yb_ref, yc_ref, w_ref, x_ref, mod_ref, ada_ref, o_ref, wbf_ref, *, gate_idx):
    @pl.when(pl.program_id(1) == 0)
    def _():
        _cast_weight(w_ref, wbf_ref)

    acc = jnp.dot(ya_ref[...], wbf_ref[0:A_WIDTH, :], preferred_element_type=F32)
    acc = acc + jnp.dot(yb_ref[...], wbf_ref[A_WIDTH:A_WIDTH + B_WIDTH, :], preferred_element_type=F32)
    acc = acc + jnp.dot(yc_ref[...], wbf_ref[A_WIDTH + B_WIDTH:, :], preferred_element_type=F32)
    o_ref[...] = x_ref[...] + _gate_row(mod_ref, ada_ref, gate_idx) * acc


def _out_proj(ya, yb, yc, w_o, x2, mod, ada_table, layer, s_len, gate_idx):
    t, d = x2.shape
    tm = min(TM, s_len)
    tn = TN
    per_b = s_len // tm
    return pl.pallas_call(
        functools.partial(_outproj_kernel, gate_idx=gate_idx),
        grid=(d // tn, t // tm),
        in_specs=[
            pl.BlockSpec((tm, A_WIDTH), lambda j, i: (i, 0)),
            pl.BlockSpec((tm, B_WIDTH), lambda j, i: (i, 0)),
            pl.BlockSpec((tm, C_WIDTH), lambda j, i: (i, 0)),
            pl.BlockSpec((None, d, tn), lambda j, i: (layer, 0, j)),
            pl.BlockSpec((tm, tn), lambda j, i: (i, j)),
            pl.BlockSpec((1, N_MOD, tn), lambda j, i: (i // per_b, 0, j)),
            pl.BlockSpec((None, N_MOD, tn), lambda j, i: (layer, 0, j)),
        ],
        out_specs=pl.BlockSpec((tm, tn), lambda j, i: (i, j)),
        out_shape=jax.ShapeDtypeStruct((t, d), F32),
        scratch_shapes=[pltpu.VMEM((d, tn), BF16)],
        compiler_params=_cparams(2),
        name="out_proj",
    )(ya, yb, yc, w_o, x2, mod, ada_table)


def _tile_flags(te_ref, n_tiles):
    i = pl.program_id(1)
    prev = te_ref[jnp.maximum(i - 1, 0)]
    first = jnp.logical_or(i == 0, te_ref[i] != prev)
    valid = i < te_ref[n_tiles]
    return first, valid


def _glu_kernel(te_ref, x_ref, wg_ref, wu_ref, o_ref, wg_bf, wu_bf, *, n_tiles, packed):
    first, valid = _tile_flags(te_ref, n_tiles)

    @pl.when(first)
    def _():
        _cast_weight(wg_ref, wg_bf)
        _cast_weight(wu_ref, wu_bf)

    @pl.when(valid)
    def _():
        if packed:
            half = x_ref.shape[1]
            lo, hi = _unpack_bf16_pairs(x_ref[...])
            g = jnp.dot(lo, wg_bf[0:half, :], preferred_element_type=F32)
            g = g + jnp.dot(hi, wg_bf[half:, :], preferred_element_type=F32)
            u = jnp.dot(lo, wu_bf[0:half, :], preferred_element_type=F32)
            u = u + jnp.dot(hi, wu_bf[half:, :], preferred_element_type=F32)
        else:
            x = x_ref[...]
            g = jnp.dot(x, wg_bf[...], preferred_element_type=F32)
            u = jnp.dot(x, wu_bf[...], preferred_element_type=F32)
        o_ref[...] = (_silu(g) * u).astype(BF16)

    @pl.when(jnp.logical_not(valid))
    def _():
        o_ref[...] = jnp.zeros_like(o_ref)


def _glu(x, wg, wu, te, widx, tm, tn, packed):
    rows, xc = x.shape
    d, f = wg.shape[-2], wg.shape[-1]
    n_tiles = rows // tm
    lead = len(wg.shape) - 3

    def w_map(j, i, te_ref):
        return tuple(widx[:lead]) + (te_ref[i], 0, j)

    w_block = (None,) * (lead + 1) + (d, tn)
    grid_spec = pltpu.PrefetchScalarGridSpec(
        num_scalar_prefetch=1,
        grid=(f // tn, n_tiles),
        in_specs=[
            pl.BlockSpec((tm, xc), lambda j, i, te_ref: (i, 0)),
            pl.BlockSpec(w_block, w_map),
            pl.BlockSpec(w_block, w_map),
        ],
        out_specs=pl.BlockSpec((tm, tn), lambda j, i, te_ref: (i, j)),
        scratch_shapes=[pltpu.VMEM((d, tn), BF16), pltpu.VMEM((d, tn), BF16)],
    )
    return pl.pallas_call(
        functools.partial(_glu_kernel, n_tiles=n_tiles, packed=packed),
        grid_spec=grid_spec,
        out_shape=jax.ShapeDtypeStruct((rows, f), BF16),
        compiler_params=_cparams(2),
        name="glu_pair",
    )(te, x, wg, wu)


def _down_res_kernel(a_ref, w_ref, r_ref, mod_ref, ada_ref, o_ref, wbf_ref, *, gate_idx):
    @pl.when(pl.program_id(1) == 0)
    def _():
        _cast_weight(w_ref, wbf_ref)

    acc = jnp.dot(a_ref[...], wbf_ref[...], preferred_element_type=F32)
    o_ref[...] = r_ref[...] + _gate_row(mod_ref, ada_ref, gate_idx) * acc


def _down_residual(a, w_down, widx, k_half, n_half, res, mod, ada_table, layer, s_len, gate_idx):
    t, d = res.shape
    kh = a.shape[1] // n_half
    tm = min(TM, s_len)
    tn = TN
    per_b = s_len // tm
    return pl.pallas_call(
        functools.partial(_down_res_kernel, gate_idx=gate_idx),
        grid=(d // tn, t // tm),
        in_specs=[
            pl.BlockSpec((tm, kh), lambda j, i: (i, k_half)),
            pl.BlockSpec((None, kh, tn), lambda j, i: (widx, k_half, j), pipeline_mode=pl.Buffered(1)),
            pl.BlockSpec((tm, tn), lambda j, i: (i, j)),
            pl.BlockSpec((1, N_MOD, tn), lambda j, i: (i // per_b, 0, j)),
            pl.BlockSpec((None, N_MOD, tn), lambda j, i: (layer, 0, j)),
        ],
        out_specs=pl.BlockSpec((tm, tn), lambda j, i: (i, j)),
        out_shape=jax.ShapeDtypeStruct((t, d), F32),
        scratch_shapes=[pltpu.VMEM((kh, tn), BF16)],
        compiler_params=_cparams(2),
        name="down_residual",
    )(a, w_down, res, mod, ada_table)


def _down_group_kernel(te_ref, a_ref, w_ref, o_ref, wbf_ref, *, n_tiles):
    first, valid = _tile_flags(te_ref, n_tiles)

    @pl.when(first)
    def _():
        _cast_weight(w_ref, wbf_ref)

    @pl.when(valid)
    def _():
        o_ref[...] = _pack_bf16_pairs(jnp.dot(a_ref[...], wbf_ref[...], preferred_element_type=F32))

    @pl.when(jnp.logical_not(valid))
    def _():
        o_ref[...] = jnp.zeros_like(o_ref)


def _down_grouped(a, w_down, moe_idx, te, tm):
    rows, k = a.shape
    d = w_down.shape[-1]
    tn = TN
    n_tiles = rows // tm
    grid_spec = pltpu.PrefetchScalarGridSpec(
        num_scalar_prefetch=1,
        grid=(d // tn, n_tiles),
        in_specs=[
            pl.BlockSpec((tm, k), lambda j, i, te_ref: (i, 0)),
            pl.BlockSpec((None, None, k, tn), lambda j, i, te_ref: (moe_idx, te_ref[i], 0, j)),
        ],
        out_specs=pl.BlockSpec((tm, tn // 2), lambda j, i, te_ref: (i, j)),
        scratch_shapes=[pltpu.VMEM((k, tn), BF16)],
    )
    return pl.pallas_call(
        functools.partial(_down_group_kernel, n_tiles=n_tiles),
        grid_spec=grid_spec,
        out_shape=jax.ShapeDtypeStruct((rows, d // 2), U32),
        compiler_params=_cparams(2),
        name="down_grouped",
    )(te, a, w_down)


def _route_kernel(lg_ref, tbl_ref, cnt_ref, carry_ref):
    i = pl.program_id(0)

    @pl.when(i == 0)
    def _():
        carry_ref[...] = jnp.zeros_like(carry_ref)

    tm = lg_ref.shape[0]
    lane = lax.broadcasted_iota(I32, (tm, LANES), 1).astype(F32)
    neg = jnp.float32(-jnp.inf)
    lg = jnp.where(lane < N_EXPERTS, lg_ref[...], neg)
    m1 = jnp.max(lg, axis=-1, keepdims=True)
    e1 = jnp.min(jnp.where(lg == m1, lane, float(LANES)), axis=-1, keepdims=True)
    lg2 = jnp.where(lane == e1, neg, lg)
    m2 = jnp.max(lg2, axis=-1, keepdims=True)
    e2 = jnp.min(jnp.where(lg2 == m2, lane, float(LANES)), axis=-1, keepdims=True)
    tt = jnp.exp(m2 - m1)
    w1 = 1.0 / (1.0 + tt)
    w2 = tt * w1

    memb = jnp.logical_or(lane == e1, lane == e2).astype(F32)
    rr = lax.broadcasted_iota(I32, (tm, tm), 0)
    cc = lax.broadcasted_iota(I32, (tm, tm), 1)
    strict_lower = (cc < rr).astype(BF16)
    pos = jnp.dot(strict_lower, memb.astype(BF16), preferred_element_type=F32) + carry_ref[...]
    p1 = jnp.sum(jnp.where(lane == e1, pos, 0.0), axis=-1, keepdims=True)
    p2 = jnp.sum(jnp.where(lane == e2, pos, 0.0), axis=-1, keepdims=True)
    carry_ref[...] = carry_ref[...] + jnp.sum(memb, axis=0, keepdims=True)

    out = jnp.where(lane == 0, e1, 0.0)
    out = jnp.where(lane == 1, e2, out)
    out = jnp.where(lane == 2, p1, out)
    out = jnp.where(lane == 3, p2, out)
    out = jnp.where(lane == 4, w1, out)
    out = jnp.where(lane == 5, w2, out)
    tbl_ref[...] = out
    cnt_ref[...] = carry_ref[...]


def _route(logits):
    t = logits.shape[0]
    tm = min(TM_ROUTE, t)
    return pl.pallas_call(
        _route_kernel,
        grid=(t // tm,),
        in_specs=[pl.BlockSpec((tm, LANES), lambda i: (i, 0))],
        out_specs=[
            pl.BlockSpec((tm, LANES), lambda i: (i, 0)),
            pl.BlockSpec((1, LANES), lambda i: (0, 0)),
        ],
        out_shape=[
            jax.ShapeDtypeStruct((t, LANES), F32),
            jax.ShapeDtypeStruct((1, LANES), F32),
        ],
        scratch_shapes=[pltpu.VMEM((1, LANES), F32)],
        compiler_params=_cparams(1),
        name="moe_route",
    )(logits)


def _dispatch_kernel(dest_ref, plan_ref, hp_ref, xs_ref, zero_ref, sem, zsem, *, tile):
    ch = hp_ref.shape[0]
    n_tiles = xs_ref.shape[0] // tile

    @pl.when(pl.program_id(0) == 0)
    def _():
        zero_ref[...] = jnp.zeros_like(zero_ref)

        def zero_block(start):
            return pltpu.make_async_copy(zero_ref, xs_ref.at[pl.ds(start, tile)], zsem)

        for e in range(N_EXPERTS):
            blk = zero_block(pl.multiple_of(jnp.maximum(plan_ref[e] - tile, 0), tile))
            blk.start()
            blk.wait()

        def tail_start(k, carry):
            zero_block(pl.multiple_of(k * tile, tile)).start()
            return carry

        def tail_wait(k, carry):
            zero_block(0).wait()
            return carry

        lax.fori_loop(plan_ref[N_EXPERTS], n_tiles, tail_start, 0)
        lax.fori_loop(plan_ref[N_EXPERTS], n_tiles, tail_wait, 0)

    def issue(r, carry):
        src = hp_ref.at[pl.ds(r, 1)]
        pltpu.make_async_copy(src, xs_ref.at[pl.ds(dest_ref[2 * r], 1)], sem).start()
        pltpu.make_async_copy(src, xs_ref.at[pl.ds(dest_ref[2 * r + 1], 1)], sem).start()
        return carry

    lax.fori_loop(0, ch, issue, 0, unroll=8)

    for _ in range(2):
        pltpu.make_async_copy(hp_ref, xs_ref.at[pl.ds(0, ch)], sem).wait()


def _dispatch(hp, dest_flat, plan, rows, tile):
    t, w = hp.shape
    ch = min(CH_DISP, t)
    return pl.pallas_call(
        functools.partial(_dispatch_kernel, tile=tile),
        grid=(t // ch,),
        in_specs=[
            pl.BlockSpec((2 * ch,), lambda i: (i,), memory_space=pltpu.SMEM),
            pl.BlockSpec((N_EXPERTS + 1,), lambda i: (0,), memory_space=pltpu.SMEM),
            pl.BlockSpec((ch, w), lambda i: (i, 0)),
        ],
        out_specs=pl.BlockSpec(memory_space=pl.ANY),
        out_shape=jax.ShapeDtypeStruct((rows, w), hp.dtype),
        scratch_shapes=[
            pltpu.VMEM((tile, w), hp.dtype),
            pltpu.SemaphoreType.DMA(()),
            pltpu.SemaphoreType.DMA(()),
        ],
        compiler_params=_cparams(1),
        name="moe_dispatch",
    )(dest_flat, plan, hp)


def _combine_kernel(dcur_ref, dnext_ref, ys_ref, x_ref, tbl_ref, mod_ref, ada_ref, o_ref, buf, sem, *, gate_idx):
    i = pl.program_id(0)
    n = pl.num_programs(0)
    tm = x_ref.shape[0]
    slot = i % 2

    def issue(d_ref, sl):
        def body(r, carry):
            for k in range(2):
                pltpu.make_async_copy(
                    ys_ref.at[pl.ds(d_ref[2 * r + k], 1)], buf.at[sl, k, pl.ds(r, 1)], sem.at[sl]).start()
            return carry

        lax.fori_loop(0, tm, body, 0, unroll=8)

    @pl.when(i == 0)
    def _():
        issue(dcur_ref, 0)

    @pl.when(i + 1 < n)
    def _():
        issue(dnext_ref, 1 - slot)

    for k in range(2):
        pltpu.make_async_copy(ys_ref.at[pl.ds(0, tm)], buf.at[slot, k], sem.at[slot]).wait()

    gate = _gate_row(mod_ref, ada_ref, gate_idx)
    hi_mask = jnp.uint32(0xFFFF0000)
    half = TN // 2
    rb = 32

    def rows(r, carry):
        sl = pl.ds(pl.multiple_of(r * rb, rb), rb)
        w1 = tbl_ref[sl, 4:5]
        w2 = tbl_ref[sl, 5:6]
        for jb in range(o_ref.shape[1] // TN):
            y1 = buf[slot, 0, sl, jb * half:(jb + 1) * half]
            y2 = buf[slot, 1, sl, jb * half:(jb + 1) * half]
            f_lo = w1 * lax.bitcast_convert_type(y1 << 16, F32) + w2 * lax.bitcast_convert_type(y2 << 16, F32)
            f_hi = (w1 * lax.bitcast_convert_type(y1 & hi_mask, F32)
                    + w2 * lax.bitcast_convert_type(y2 & hi_mask, F32))
            for part, f in ((0, f_lo), (1, f_hi)):
                cols = slice(jb * TN + part * half, jb * TN + (part + 1) * half)
                o_ref[sl, cols] = x_ref[sl, cols] + gate[:, cols] * f
        return carry

    lax.fori_loop(0, tm // rb, rows, 0)


def _combine(ys, dest_flat, x2, tbl, mod, ada_table, layer, s_len, gate_idx):
    t, d = x2.shape
    tm = TM_COMB
    per_b = s_len // tm
    n = t // tm
    return pl.pallas_call(
        functools.partial(_combine_kernel, gate_idx=gate_idx),
        grid=(n,),
        in_specs=[
            pl.BlockSpec((2 * tm,), lambda i: (i,), memory_space=pltpu.SMEM),
            pl.BlockSpec((2 * tm,), lambda i: (jnp.minimum(i + 1, n - 1),), memory_space=pltpu.SMEM),
            pl.BlockSpec(memory_space=pl.ANY),
            pl.BlockSpec((tm, d), lambda i: (i, 0)),
            pl.BlockSpec((tm, LANES), lambda i: (i, 0)),
            pl.BlockSpec((1, N_MOD, d), lambda i: (i // per_b, 0, 0)),
            pl.BlockSpec((None, N_MOD, d), lambda i: (layer, 0, 0)),
        ],
        out_specs=pl.BlockSpec((tm, d), lambda i: (i, 0)),
        out_shape=jax.ShapeDtypeStruct((t, d), F32),
        scratch_shapes=[
            pltpu.VMEM((2, 2, tm, d // 2), U32),
            pltpu.SemaphoreType.DMA((2,)),
        ],
        compiler_params=_cparams(1),
        name="moe_combine",
    )(dest_flat, dest_flat, ys, x2, tbl, mod, ada_table)


def _moe_plan(tbl, cnt, tm, n_tiles):
    e = tbl[:, 0:2].astype(I32)
    pos = tbl[:, 2:4].astype(I32)
    counts = cnt[0, :N_EXPERTS].astype(I32)
    padded = ((counts + tm - 1) // tm) * tm
    ends = jnp.cumsum(padded)
    offs = ends - padded
    expert_ids = jnp.arange(N_EXPERTS, dtype=I32)
    dest = jnp.sum(jnp.where(e[..., None] == expert_ids, offs, 0), axis=-1) + pos
    tile_start = jnp.arange(n_tiles, dtype=I32) * tm
    n_used = ends[-1] // tm
    te = jnp.sum((tile_start[:, None] >= ends[None, :]).astype(I32), axis=1)
    last = jnp.minimum(jnp.maximum(n_used - 1, 0), n_tiles - 1)
    te = jnp.where(jnp.arange(n_tiles) < n_used, jnp.minimum(te, N_EXPERTS - 1), te[last])
    te = jnp.concatenate([te.astype(I32), n_used.astype(I32)[None]])
    plan = jnp.concatenate([ends.astype(I32), n_used.astype(I32)[None]])
    return dest.reshape(-1), te, plan


def kernel(x, c, w_ada, b_ada, ada_table, norm_mix_w, w_in, lb_logits, hgrn_norm_w, q_norm_w, k_norm_w,
           rel_bias, w_pool, pool_scale, w_o, norm_ffn_w, ffn_w_gate, ffn_w_up, ffn_w_down, moe_w_router,
           moe_b_router, moe_w_gate, moe_w_up, moe_w_down):
    bsz, s_len, d = x.shape
    depth = w_in.shape[0]
    t = bsz * s_len
    x2 = x.reshape(t, d)

    mod = _adaln_mod(c, w_ada, b_ada).reshape(bsz, N_MOD, d)

    lb_p = jax.nn.softmax(lb_logits.astype(F32), axis=0)
    lb_all = (jnp.cumsum(lb_p, axis=0) - lb_p[0:1]).reshape(depth, A_HEADS, 1, HEAD_DIM)
    bias = _group_bias(rel_bias)
    pool_scale3 = pool_scale.reshape(depth, 1, C_WIDTH)
    n_moe = moe_w_router.shape[0]
    w_router_pad = jnp.pad(moe_w_router, ((0, 0), (0, 0), (0, LANES - N_EXPERTS)))
    b_router_pad = jnp.pad(moe_b_router, ((0, 0), (0, LANES - N_EXPERTS))).reshape(n_moe, 1, LANES)
    tm_dense = min(TM, s_len)
    dense_te = jnp.concatenate([jnp.zeros((t // tm_dense,), I32), jnp.full((1,), t // tm_dense, I32)])

    for l in range(depth):
        h = _norm(x2, mod, ada_table, norm_mix_w, l, s_len, 0)
        proj = _in_proj(h, w_in, l, bsz, s_len)
        y_a, y_b = _hgrn_attention(proj, lb_all[l], hgrn_norm_w[l].reshape(1, HEAD_DIM), bias,
                                   q_norm_w[l].reshape(1, HEAD_DIM), k_norm_w[l].reshape(1, HEAD_DIM))
        y_c = _pool_mixer(proj, w_pool, pool_scale3, l)
        x2 = _out_proj(y_a.reshape(t, A_WIDTH), y_b.reshape(t, B_WIDTH), y_c.reshape(t, C_WIDTH),
                       w_o, x2, mod, ada_table, l, s_len, 2)
        if l % 2 == 0:
            j = l // 2
            h = _norm(x2, mod, ada_table, norm_ffn_w, l, s_len, 3)
            a = _glu(h, ffn_w_gate, ffn_w_up, dense_te, (), tm_dense, TN_GLU, packed=False)
            x2 = _down_residual(a, ffn_w_down, j, 0, 2, x2, mod, ada_table, l, s_len, 5)
            x2 = _down_residual(a, ffn_w_down, j, 1, 2, x2, mod, ada_table, l, s_len, 5)
        else:
            j = l // 2
            hp, logits = _norm_router(x2, mod, ada_table, norm_ffn_w, w_router_pad, b_router_pad, l, j, s_len, 3)
            tbl, cnt = _route(logits)
            rows = 2 * t + N_EXPERTS * TM_E
            n_tiles = rows // TM_E
            dest, te, plan = _moe_plan(tbl, cnt, TM_E, n_tiles)
            xs = _dispatch(hp, dest, plan, rows, TM_E)
            a = _glu(xs, moe_w_gate, moe_w_up, te, (j,), TM_E, TN_GLU_E, packed=True)
            ys = _down_grouped(a, moe_w_down, j, te, TM_E)
            x2 = _combine(ys, dest, x2, tbl, mod, ada_table, l, s_len, 5)
    return x2.reshape(bsz, s_len, d)
```

```python
import functools

import jax
import jax.numpy as jnp
from jax import lax
from jax.experimental import pallas as pl
from jax.experimental.pallas import tpu as pltpu

F32 = jnp.float32
BF16 = jnp.bfloat16
U32 = jnp.uint32
I32 = jnp.int32

D_MODEL = 4096
CHUNK = 64
HEAD_DIM = 128
A_HEADS = 12
B_HEADS = 12
A_WIDTH = A_HEADS * HEAD_DIM
B_WIDTH = B_HEADS * HEAD_DIM
C_WIDTH = D_MODEL - A_WIDTH - B_WIDTH
POOL_WINDOWS = (2, 4, 8, 16)
POOL_GROUP = C_WIDTH // len(POOL_WINDOWS)
IN_COLS = 4 * A_WIDTH + 3 * B_WIDTH + C_WIDTH
IN_HEADS = IN_COLS // HEAD_DIM
BAND_CHUNKS = 9
BAND = BAND_CHUNKS * CHUNK
BAND_PAD = (BAND_CHUNKS - 1) * CHUNK
MAX_REL_DIST = 128
N_EXPERTS = 8
N_MOD = 6
EPS = 1e-6
F_MIN = 1e-6
MASK_VALUE = -1e9

LANES = 128
SUBLANES = 8
VMEM_LIMIT_MB = 56

TM = 1024
TN = 512
TN_GLU = 256
TN_GLU_E = 512
TM_E = 512
TM_NORM = 512
TM_ROUTE = 512
TM_COMB = 256
CH_DISP = 256


def _cparams(n_grid, vmem_mb=VMEM_LIMIT_MB):
    return pltpu.CompilerParams(
        dimension_semantics=("arbitrary",) * n_grid,
        vmem_limit_bytes=vmem_mb * 1024 * 1024,
    )


def _silu(x):
    return x * jax.nn.sigmoid(x)


def _cast_weight(w_ref, wbf_ref):
    k = w_ref.shape[0]
    rows = 512 if k % 512 == 0 else 128
    assert k % rows == 0

    def body(r, carry):
        sl = pl.ds(pl.multiple_of(r * rows, rows), rows)
        wbf_ref[sl, :] = w_ref[sl, :].astype(BF16)
        return carry

    lax.fori_loop(0, k // rows, body, 0)


def _mod_kernel(c_ref, w_ref, b_ref, o_ref):
    s = _silu(c_ref[...]).astype(BF16)
    w = w_ref[...].astype(BF16)
    o_ref[...] = jnp.dot(s, w, preferred_element_type=F32) + b_ref[...]


def _adaln_mod(c, w_ada, b_ada):
    bsz, d = c.shape
    n = w_ada.shape[1]
    tn = 1024
    return pl.pallas_call(
        _mod_kernel,
        grid=(n // tn,),
        in_specs=[
            pl.BlockSpec((bsz, d), lambda j: (0, 0)),
            pl.BlockSpec((d, tn), lambda j: (0, j)),
            pl.BlockSpec((1, tn), lambda j: (0, j)),
        ],
        out_specs=pl.BlockSpec((bsz, tn), lambda j: (0, j)),
        out_shape=jax.ShapeDtypeStruct((bsz, n), F32),
        compiler_params=_cparams(1),
        name="adaln_mod",
    )(c, w_ada, b_ada.reshape(1, n))


def _modulated_norm(x_ref, mod_ref, ada_ref, w_ref, shift_idx):
    x = x_ref[...]
    y = x * lax.rsqrt(jnp.mean(x * x, axis=-1, keepdims=True) + EPS) * w_ref[...]
    m = mod_ref[0] + ada_ref[...]
    shift = m[shift_idx:shift_idx + 1]
    scale = m[shift_idx + 1:shift_idx + 2]
    return y * (1.0 + scale) + shift


def _norm_kernel(x_ref, mod_ref, ada_ref, w_ref, h_ref, *, shift_idx):
    h_ref[...] = _modulated_norm(x_ref, mod_ref, ada_ref, w_ref, shift_idx).astype(BF16)


def _pack_bf16_pairs(h):
    n = h.shape[1] // 2
    lo = lax.bitcast_convert_type(h[:, :n].astype(BF16).astype(F32), U32)
    hi = lax.bitcast_convert_type(h[:, n:].astype(BF16).astype(F32), U32)
    return (hi & jnp.uint32(0xFFFF0000)) | (lo >> 16)


def _unpack_bf16_pairs(u):
    lo = lax.bitcast_convert_type(u << 16, F32).astype(BF16)
    hi = lax.bitcast_convert_type(u & jnp.uint32(0xFFFF0000), F32).astype(BF16)
    return lo, hi


def _norm_router_kernel(x_ref, mod_ref, ada_ref, w_ref, wr_ref, br_ref, hp_ref, lg_ref, *, shift_idx):
    h = _modulated_norm(x_ref, mod_ref, ada_ref, w_ref, shift_idx)
    hp_ref[...] = _pack_bf16_pairs(h)
    wr = wr_ref[...]
    h_hi = h.astype(BF16)
    h_lo = (h - h_hi.astype(F32)).astype(BF16)
    w_hi = wr.astype(BF16)
    w_lo = (wr - w_hi.astype(F32)).astype(BF16)
    lg = jnp.dot(h_hi, w_hi, preferred_element_type=F32)
    lg = lg + jnp.dot(h_hi, w_lo, preferred_element_type=F32)
    lg = lg + jnp.dot(h_lo, w_hi, preferred_element_type=F32)
    lg_ref[...] = lg + br_ref[...]


def _norm_specs(t, d, s_len, tm, layer):
    per_b = s_len // tm
    return [
        pl.BlockSpec((tm, d), lambda i: (i, 0)),
        pl.BlockSpec((1, N_MOD, d), lambda i: (i // per_b, 0, 0)),
        pl.BlockSpec((None, N_MOD, d), lambda i: (layer, 0, 0)),
        pl.BlockSpec((None, 1, d), lambda i: (layer, 0, 0)),
    ]


def _norm(x2, mod, ada_table, norm_w, layer, s_len, shift_idx):
    t, d = x2.shape
    tm = min(TM_NORM, s_len)
    return pl.pallas_call(
        functools.partial(_norm_kernel, shift_idx=shift_idx),
        grid=(t // tm,),
        in_specs=_norm_specs(t, d, s_len, tm, layer),
        out_specs=pl.BlockSpec((tm, d), lambda i: (i, 0)),
        out_shape=jax.ShapeDtypeStruct((t, d), BF16),
        compiler_params=_cparams(1),
        name="norm_mod",
    )(x2, mod, ada_table, norm_w.reshape(norm_w.shape[0], 1, d))


def _norm_router(x2, mod, ada_table, norm_w, w_router_pad, b_router_pad, layer, moe_idx, s_len, shift_idx):
    t, d = x2.shape
    tm = min(TM_NORM, s_len)
    return pl.pallas_call(
        functools.partial(_norm_router_kernel, shift_idx=shift_idx),
        grid=(t // tm,),
        in_specs=_norm_specs(t, d, s_len, tm, layer) + [
            pl.BlockSpec((None, d, LANES), lambda i: (moe_idx, 0, 0)),
            pl.BlockSpec((None, 1, LANES), lambda i: (moe_idx, 0, 0)),
        ],
        out_specs=[
            pl.BlockSpec((tm, d // 2), lambda i: (i, 0)),
            pl.BlockSpec((tm, LANES), lambda i: (i, 0)),
        ],
        out_shape=[
            jax.ShapeDtypeStruct((t, d // 2), U32),
            jax.ShapeDtypeStruct((t, LANES), F32),
        ],
        compiler_params=_cparams(1),
        name="norm_mod_router",
    )(x2, mod, ada_table, norm_w.reshape(norm_w.shape[0], 1, d), w_router_pad, b_router_pad)


def _inproj_kernel(h_ref, w_ref, o_ref, wbf_ref):
    @pl.when(pl.program_id(1) == 0)
    def _():
        _cast_weight(w_ref, wbf_ref)

    acc = jnp.dot(h_ref[...], wbf_ref[...], preferred_element_type=F32)
    for hh in range(o_ref.shape[1]):
        o_ref[0, hh] = acc[:, hh * HEAD_DIM:(hh + 1) * HEAD_DIM].astype(BF16)


def _in_proj(h, w_in, layer, bsz, s_len):
    t, d = h.shape
    n = w_in.shape[2]
    tm = min(TM, s_len)
    tn = TN
    per_b = s_len // tm
    hpt = tn // HEAD_DIM
    return pl.pallas_call(
        _inproj_kernel,
        grid=(n // tn, t // tm),
        in_specs=[
            pl.BlockSpec((tm, d), lambda j, i: (i, 0)),
            pl.BlockSpec((None, d, tn), lambda j, i: (layer, 0, j)),
        ],
        out_specs=pl.BlockSpec((1, hpt, tm, HEAD_DIM), lambda j, i: (i // per_b, j, i % per_b, 0)),
        out_shape=jax.ShapeDtypeStruct((bsz, n // HEAD_DIM, s_len, HEAD_DIM), BF16),
        scratch_shapes=[pltpu.VMEM((d, tn), BF16)],
        compiler_params=_cparams(2),
        name="in_proj",
    )(h, w_in)


_HGRN_LEVELS = (32, 16, 8, 4, 2, 1)
_ATTN_GROUP = 4
_ATTN_QROWS = _ATTN_GROUP * CHUNK
_ATTN_KROWS = BAND_PAD + _ATTN_QROWS


def _row_bcast(x, period, row):
    n, c = x.shape
    x3 = x.reshape(n // period, period, c)
    return jnp.broadcast_to(x3[:, row:row + 1, :], x3.shape).reshape(n, c)


def _hgrn_chunk_fn(q_ref, z_ref, v_ref, g_ref, lb_ref, nw_ref, o_ref, st_ref):
    lb = lb_ref[0]
    nw = nw_ref[...]
    row = lax.broadcasted_iota(I32, (CHUNK, HEAD_DIM), 0)
    sub3 = lax.broadcasted_iota(I32, (CHUNK // SUBLANES, SUBLANES, HEAD_DIM), 1)
    r64 = lax.broadcasted_iota(I32, (CHUNK, CHUNK), 0)
    c64 = lax.broadcasted_iota(I32, (CHUNK, CHUNK), 1)
    level_masks = []
    for hs in _HGRN_LEVELS:
        pair = 2 * hs
        same = (r64 & ~(pair - 1)) == (c64 & ~(pair - 1))
        level_masks.append(same & ((r64 & (pair - 1)) >= hs) & ((c64 & (pair - 1)) < hs))

    st_ref[...] = jnp.zeros_like(st_ref)

    def chunk(ci, carry):
        sl = pl.ds(pl.multiple_of(ci * CHUNK, CHUNK), CHUNK)
        q_bf = q_ref[0, 0, sl, :]
        q = q_bf.astype(F32)
        z = z_ref[0, 0, sl, :].astype(F32)
        v_bf = v_ref[0, 0, sl, :]
        g = g_ref[0, 0, sl, :].astype(F32)

        e = jnp.exp(-jnp.abs(z))
        r = 1.0 / (1.0 + e)
        sig_pos = jnp.where(z >= 0, r, e * r)
        sig_neg = jnp.where(z >= 0, e * r, r)
        f_gate = lb + (1.0 - lb) * sig_pos
        log_f = jnp.log2(jnp.maximum(f_gate, F_MIN))
        key = (1.0 - lb) * sig_neg

        grp = CHUNK // SUBLANES
        b3 = log_f.reshape(grp, SUBLANES, HEAD_DIM)
        for sh in (1, 2, 4):
            b3 = b3 + jnp.where(sub3 >= sh, pltpu.roll(b3, sh, axis=1), 0.0)
        run = jnp.zeros((1, 1, HEAD_DIM), F32)
        parts = []
        for gidx in range(grp):
            parts.append(b3[gidx:gidx + 1] + run)
            run = run + b3[gidx:gidx + 1, SUBLANES - 1:SUBLANES, :]
        b = jnp.concatenate(parts, axis=0).reshape(CHUNK, HEAD_DIM)
        b_last = run.reshape(1, HEAD_DIM)

        m4 = row & 3
        key_bf = key.astype(BF16)
        scores = jnp.zeros((CHUNK, CHUNK), F32)
        for hs, mask in zip(_HGRN_LEVELS, level_masks):
            if hs >= 4:
                x = -jnp.abs(b - _row_bcast(b, 2 * hs, hs - 1))
            elif hs == 2:
                nxt = pltpu.roll(log_f, CHUNK - 1, axis=0)
                prv = pltpu.roll(log_f, 1, axis=0)
                x = jnp.where(m4 == 0, nxt, jnp.where(m4 == 1, 0.0, jnp.where(m4 == 2, log_f, log_f + prv)))
            else:
                x = jnp.where((row & 1) == 1, log_f, 0.0)
            w = jnp.exp2(x)
            a_l = (q * w).astype(BF16)
            k_l = (key * w).astype(BF16)
            s_l = lax.dot_general(a_l, k_l, (((1,), (1,)), ((), ())), preferred_element_type=F32)
            scores = scores + jnp.where(mask, s_l, 0.0)
        s_d = lax.dot_general(q_bf, key_bf, (((1,), (1,)), ((), ())), preferred_element_type=F32)
        scores = scores + jnp.where(r64 == c64, s_d, 0.0)
        intra = jnp.dot(scores.astype(BF16), v_bf, preferred_element_type=F32)

        st = st_ref[...]
        qe = (q * jnp.exp2(b)).astype(BF16)
        inter = lax.dot_general(qe, st.astype(BF16), (((1,), (1,)), ((), ())), preferred_element_type=F32)
        kd = (key * jnp.exp2(b_last - b)).astype(BF16)
        upd = lax.dot_general(v_bf, kd, (((0,), (0,)), ((), ())), preferred_element_type=F32)
        st_ref[...] = st * jnp.exp2(b_last) + upd

        out = intra + inter
        y = out * lax.rsqrt(jnp.mean(out * out, axis=-1, keepdims=True) + EPS) * nw
        o_ref[0, sl, :] = (y * _silu(g)).astype(BF16)
        return carry

    return chunk


def _head_rms(x, w):
    return x * lax.rsqrt(jnp.mean(x * x, axis=-1, keepdims=True) + EPS) * w


def _attn_group_fn(q_ref, k_ref, v_ref, bias_ref, qw_ref, kw_ref, o_ref, kp_ref, vp_ref):
    s_len = q_ref.shape[2]
    qw = qw_ref[...]
    kw = kw_ref[...]
    scale = HEAD_DIM ** -0.5
    col = lax.broadcasted_iota(I32, (_ATTN_QROWS, _ATTN_KROWS), 1)

    kp_ref[0:BAND_PAD, :] = jnp.zeros((BAND_PAD, HEAD_DIM), BF16)
    vp_ref[0:BAND_PAD, :] = jnp.zeros((BAND_PAD, HEAD_DIM), BF16)
    blk = _ATTN_QROWS

    def prep(i, carry):
        src = pl.ds(pl.multiple_of(i * blk, blk), blk)
        dst = pl.ds(pl.multiple_of(BAND_PAD + i * blk, CHUNK), blk)
        kp_ref[dst, :] = _head_rms(k_ref[0, 0, src, :].astype(F32), kw).astype(BF16)
        vp_ref[dst, :] = v_ref[0, 0, src, :]
        return carry

    lax.fori_loop(0, s_len // blk, prep, 0, unroll=2 if (s_len // blk) % 2 == 0 else 1)

    def group(gi, carry):
        r0 = pl.multiple_of(gi * _ATTN_QROWS, _ATTN_QROWS)
        qn = _head_rms(q_ref[0, 0, pl.ds(r0, _ATTN_QROWS), :].astype(F32), qw).astype(BF16)
        kc = kp_ref[pl.ds(r0, _ATTN_KROWS), :]
        vc = vp_ref[pl.ds(r0, _ATTN_KROWS), :]
        s = lax.dot_general(qn, kc, (((1,), (1,)), ((), ())), preferred_element_type=F32) * scale + bias_ref[0]
        s = jnp.where(col + r0 >= BAND_PAD, s, MASK_VALUE)
        m = jnp.max(s, axis=-1, keepdims=True)
        p = jnp.exp(s - m)
        den = jnp.sum(p, axis=-1, keepdims=True)
        o = jnp.dot(p.astype(BF16), vc, preferred_element_type=F32) / den
        o_ref[0, pl.ds(r0, _ATTN_QROWS), :] = o.astype(BF16)
        return carry

    return group


def _mixers_kernel(qa_ref, za_ref, va_ref, ga_ref, lb_ref, nw_ref, qb_ref, kb_ref, vb_ref, bias_ref, qw_ref, kw_ref,
                   oa_ref, ob_ref, st_ref, kp_ref, vp_ref):
    s_len = qa_ref.shape[2]
    chunk = _hgrn_chunk_fn(qa_ref, za_ref, va_ref, ga_ref, lb_ref, nw_ref, oa_ref, st_ref)
    group = _attn_group_fn(qb_ref, kb_ref, vb_ref, bias_ref, qw_ref, kw_ref, ob_ref, kp_ref, vp_ref)

    def body(gi, carry):
        for u in range(_ATTN_GROUP):
            chunk(gi * _ATTN_GROUP + u, carry)
        group(gi, carry)
        return carry

    lax.fori_loop(0, s_len // _ATTN_QROWS, body, 0, unroll=2)


def _hgrn_attention(proj_hm, lb_l, norm_w_l, bias_grp, qn_w_l, kn_w_l):
    bsz, _, s_len, _ = proj_hm.shape
    assert s_len % _ATTN_QROWS == 0 and A_HEADS == B_HEADS

    def slab(off):
        return pl.BlockSpec((1, 1, s_len, HEAD_DIM), lambda h, b: (b, off + h, 0, 0))

    vec = pl.BlockSpec((1, HEAD_DIM), lambda h, b: (0, 0))
    out = pl.BlockSpec((1, s_len, HEAD_DIM), lambda h, b: (b, 0, h))
    base = 4 * A_HEADS
    return pl.pallas_call(
        _mixers_kernel,
        grid=(A_HEADS, bsz),
        in_specs=[
            slab(0), slab(A_HEADS), slab(2 * A_HEADS), slab(3 * A_HEADS),
            pl.BlockSpec((1, 1, HEAD_DIM), lambda h, b: (h, 0, 0)),
            vec,
            slab(base), slab(base + B_HEADS), slab(base + 2 * B_HEADS),
            pl.BlockSpec((1, _ATTN_QROWS, _ATTN_KROWS), lambda h, b: (h, 0, 0)),
            vec, vec,
        ],
        out_specs=[out, out],
        out_shape=[
            jax.ShapeDtypeStruct((bsz, s_len, A_WIDTH), BF16),
            jax.ShapeDtypeStruct((bsz, s_len, B_WIDTH), BF16),
        ],
        scratch_shapes=[
            pltpu.VMEM((HEAD_DIM, HEAD_DIM), F32),
            pltpu.VMEM((BAND_PAD + s_len, HEAD_DIM), BF16),
            pltpu.VMEM((BAND_PAD + s_len, HEAD_DIM), BF16),
        ],
        compiler_params=_cparams(2),
        name="hgrn_attention",
    )(proj_hm, proj_hm, proj_hm, proj_hm, lb_l, norm_w_l, proj_hm, proj_hm, proj_hm, bias_grp, qn_w_l, kn_w_l)


def _group_bias(rel_bias):
    diag = jnp.arange(-(CHUNK - 1), BAND)
    idx = jnp.clip(BAND_PAD - diag, -MAX_REL_DIST, MAX_REL_DIST) + MAX_REL_DIST
    per_diag = jnp.take(rel_bias.astype(F32), idx, axis=1)
    bias = jnp.stack([per_diag[:, CHUNK - 1 - c:CHUNK - 1 - c + BAND] for c in range(CHUNK)], axis=1)
    blocks = [
        jnp.pad(bias, ((0, 0), (0, 0), (a * CHUNK, _ATTN_KROWS - BAND - a * CHUNK)), constant_values=MASK_VALUE)
        for a in range(_ATTN_GROUP)
    ]
    return jnp.concatenate(blocks, axis=1)


_POOL_HALO = 16


def _pool_kernel(p_ref, w_ref, sc_ref, o_ref, pad_ref, wbf_ref):
    s_len = p_ref.shape[2]
    gi = pl.program_id(1)
    blk = 256 if s_len % 256 == 0 else CHUNK
    nblk = s_len // blk
    hp = p_ref.shape[1]
    pad_ref[0:_POOL_HALO, :] = jnp.zeros((_POOL_HALO, pad_ref.shape[1]), F32)
    for hh in range(hp):
        pad_ref[_POOL_HALO:_POOL_HALO + s_len, hh * HEAD_DIM:(hh + 1) * HEAD_DIM] = p_ref[0, hh].astype(F32)
    wbf_ref[...] = w_ref[0].astype(BF16)
    scale = sc_ref[...]

    for g, win in enumerate(POOL_WINDOWS):
        @pl.when(gi == g)
        def _(win=win):
            def body(i, carry):
                r0 = pl.multiple_of(i * blk, blk)
                acc = pad_ref[pl.ds(r0, _POOL_HALO + blk), :]
                cur = acc[_POOL_HALO:, :]
                span = 1
                while span < win:
                    acc = acc + pltpu.roll(acc, span, axis=0)
                    span *= 2
                acc = acc[_POOL_HALO:, :]
                pos = lax.broadcasted_iota(I32, acc.shape, 0) + r0 + 1
                cnt = jnp.minimum(pos, win).astype(F32)
                mixed = (acc / cnt - cur).astype(BF16)
                out = jnp.dot(mixed, wbf_ref[...], preferred_element_type=F32) * scale
                o_ref[0, pl.ds(r0, blk), :] = out.astype(BF16)
                return carry

            lax.fori_loop(0, nblk, body, 0)


def _pool_mixer(proj_hm, w_pool, pool_scale3, layer):
    bsz, _, s_len, _ = proj_hm.shape
    hp = POOL_GROUP // HEAD_DIM
    base = (4 * A_HEADS + 3 * B_HEADS) // hp
    n_grp = len(POOL_WINDOWS)
    return pl.pallas_call(
        _pool_kernel,
        grid=(bsz, n_grp),
        in_specs=[
            pl.BlockSpec((1, hp, s_len, HEAD_DIM), lambda b, g: (b, base + g, 0, 0)),
            pl.BlockSpec((None, 1, POOL_GROUP, POOL_GROUP), lambda b, g: (layer, g, 0, 0)),
            pl.BlockSpec((None, 1, POOL_GROUP), lambda b, g: (layer, 0, g)),
        ],
        out_specs=pl.BlockSpec((1, s_len, POOL_GROUP), lambda b, g: (b, 0, g)),
        out_shape=jax.ShapeDtypeStruct((bsz, s_len, C_WIDTH), BF16),
        scratch_shapes=[
            pltpu.VMEM((_POOL_HALO + s_len, POOL_GROUP), F32),
            pltpu.VMEM((POOL_GROUP, POOL_GROUP), BF16),
        ],
        compiler_params=_cparams(2),
        name="pool_mixer",
    )(proj_hm, w_pool, pool_scale3)


def _gate_row(mod_ref, ada_ref, gate_idx):
    return mod_ref[0, gate_idx:gate_idx + 1, :] + ada_ref[gate_idx:gate_idx + 1, :]


def _outproj_kernel(ya_ref, yb_ref, yc_ref, w_ref, x_ref, mod_ref, ada_ref, o_ref, wbf_ref, *, gate_idx):
    @pl.when(pl.program_id(1) == 0)
    def _():
        _cast_weight(w_ref, wbf_ref)

    acc = jnp.dot(ya_ref[...], wbf_ref[0:A_WIDTH, :], preferred_element_type=F32)
    acc = acc + jnp.dot(yb_ref[...], wbf_ref[A_WIDTH:A_WIDTH + B_WIDTH, :], preferred_element_type=F32)
    acc = acc + jnp.dot(yc_ref[...], wbf_ref[A_WIDTH + B_WIDTH:, :], preferred_element_type=F32)
    o_ref[...] = x_ref[...] + _gate_row(mod_ref, ada_ref, gate_idx) * acc


def _out_proj(ya, yb, yc, w_o, x2, mod, ada_table, layer, s_len, gate_idx):
    t, d = x2.shape
    tm = min(TM, s_len)
    tn = TN
    per_b = s_len // tm
    return pl.pallas_call(
        functools.partial(_outproj_kernel, gate_idx=gate_idx),
        grid=(d // tn, t // tm),
        in_specs=[
            pl.BlockSpec((tm, A_WIDTH), lambda j, i: (i, 0)),
            pl.BlockSpec((tm, B_WIDTH), lambda j, i: (i, 0)),
            pl.BlockSpec((tm, C_WIDTH), lambda j, i: (i, 0)),
            pl.BlockSpec((None, d, tn), lambda j, i: (layer, 0, j)),
            pl.BlockSpec((tm, tn), lambda j, i: (i, j)),
            pl.BlockSpec((1, N_MOD, tn), lambda j, i: (i // per_b, 0, j)),
            pl.BlockSpec((None, N_MOD, tn), lambda j, i: (layer, 0, j)),
        ],
        out_specs=pl.BlockSpec((tm, tn), lambda j, i: (i, j)),
        out_shape=jax.ShapeDtypeStruct((t, d), F32),
        scratch_shapes=[pltpu.VMEM((d, tn), BF16)],
        compiler_params=_cparams(2),
        name="out_proj",
    )(ya, yb, yc, w_o, x2, mod, ada_table)


def _tile_flags(te_ref, n_tiles):
    i = pl.program_id(1)
    prev = te_ref[jnp.maximum(i - 1, 0)]
    first = jnp.logical_or(i == 0, te_ref[i] != prev)
    valid = i < te_ref[n_tiles]
    return first, valid


def _glu_kernel(te_ref, x_ref, wg_ref, wu_ref, o_ref, wg_bf, wu_bf, *, n_tiles, packed):
    first, valid = _tile_flags(te_ref, n_tiles)

    @pl.when(first)
    def _():
        _cast_weight(wg_ref, wg_bf)
        _cast_weight(wu_ref, wu_bf)

    @pl.when(valid)
    def _():
        if packed:
            half = x_ref.shape[1]
            lo, hi = _unpack_bf16_pairs(x_ref[...])
            g = jnp.dot(lo, wg_bf[0:half, :], preferred_element_type=F32)
            g = g + jnp.dot(hi, wg_bf[half:, :], preferred_element_type=F32)
            u = jnp.dot(lo, wu_bf[0:half, :], preferred_element_type=F32)
            u = u + jnp.dot(hi, wu_bf[half:, :], preferred_element_type=F32)
        else:
            x = x_ref[...]
            g = jnp.dot(x, wg_bf[...], preferred_element_type=F32)
            u = jnp.dot(x, wu_bf[...], preferred_element_type=F32)
        o_ref[...] = (_silu(g) * u).astype(BF16)

    @pl.when(jnp.logical_not(valid))
    def _():
        o_ref[...] = jnp.zeros_like(o_ref)


def _glu(x, wg, wu, te, widx, tm, tn, packed):
    rows, xc = x.shape
    d, f = wg.shape[-2], wg.shape[-1]
    n_tiles = rows // tm
    lead = len(wg.shape) - 3

    def w_map(j, i, te_ref):
        return tuple(widx[:lead]) + (te_ref[i], 0, j)

    w_block = (None,) * (lead + 1) + (d, tn)
    grid_spec = pltpu.PrefetchScalarGridSpec(
        num_scalar_prefetch=1,
        grid=(f // tn, n_tiles),
        in_specs=[
            pl.BlockSpec((tm, xc), lambda j, i, te_ref: (i, 0)),
            pl.BlockSpec(w_block, w_map),
            pl.BlockSpec(w_block, w_map),
        ],
        out_specs=pl.BlockSpec((tm, tn), lambda j, i, te_ref: (i, j)),
        scratch_shapes=[pltpu.VMEM((d, tn), BF16), pltpu.VMEM((d, tn), BF16)],
    )
    return pl.pallas_call(
        functools.partial(_glu_kernel, n_tiles=n_tiles, packed=packed),
        grid_spec=grid_spec,
        out_shape=jax.ShapeDtypeStruct((rows, f), BF16),
        compiler_params=_cparams(2),
        name="glu_pair",
    )(te, x, wg, wu)


def _down_res_kernel(a_ref, w_ref, r_ref, mod_ref, ada_ref, o_ref, wbf_ref, *, gate_idx):
    @pl.when(pl.program_id(1) == 0)
    def _():
        _cast_weight(w_ref, wbf_ref)

    acc = jnp.dot(a_ref[...], wbf_ref[...], preferred_element_type=F32)
    o_ref[...] = r_ref[...] + _gate_row(mod_ref, ada_ref, gate_idx) * acc


def _down_residual(a, w_down, widx, k_half, n_half, res, mod, ada_table, layer, s_len, gate_idx):
    t, d = res.shape
    kh = a.shape[1] // n_half
    tm = min(TM, s_len)
    tn = TN
    per_b = s_len // tm
    return pl.pallas_call(
        functools.partial(_down_res_kernel, gate_idx=gate_idx),
        grid=(d // tn, t // tm),
        in_specs=[
            pl.BlockSpec((tm, kh), lambda j, i: (i, k_half)),
            pl.BlockSpec((None, kh, tn), lambda j, i: (widx, k_half, j), pipeline_mode=pl.Buffered(1)),
            pl.BlockSpec((tm, tn), lambda j, i: (i, j)),
            pl.BlockSpec((1, N_MOD, tn), lambda j, i: (i // per_b, 0, j)),
            pl.BlockSpec((None, N_MOD, tn), lambda j, i: (layer, 0, j)),
        ],
        out_specs=pl.BlockSpec((tm, tn), lambda j, i: (i, j)),
        out_shape=jax.ShapeDtypeStruct((t, d), F32),
        scratch_shapes=[pltpu.VMEM((kh, tn), BF16)],
        compiler_params=_cparams(2),
        name="down_residual",
    )(a, w_down, res, mod, ada_table)


def _down_group_kernel(te_ref, a_ref, w_ref, o_ref, wbf_ref, *, n_tiles):
    first, valid = _tile_flags(te_ref, n_tiles)

    @pl.when(first)
    def _():
        _cast_weight(w_ref, wbf_ref)

    @pl.when(valid)
    def _():
        o_ref[...] = _pack_bf16_pairs(jnp.dot(a_ref[...], wbf_ref[...], preferred_element_type=F32))

    @pl.when(jnp.logical_not(valid))
    def _():
        o_ref[...] = jnp.zeros_like(o_ref)


def _down_grouped(a, w_down, moe_idx, te, tm):
    rows, k = a.shape
    d = w_down.shape[-1]
    tn = TN
    n_tiles = rows // tm
    grid_spec = pltpu.PrefetchScalarGridSpec(
        num_scalar_prefetch=1,
        grid=(d // tn, n_tiles),
        in_specs=[
            pl.BlockSpec((tm, k), lambda j, i, te_ref: (i, 0)),
            pl.BlockSpec((None, None, k, tn), lambda j, i, te_ref: (moe_idx, te_ref[i], 0, j)),
        ],
        out_specs=pl.BlockSpec((tm, tn // 2), lambda j, i, te_ref: (i, j)),
        scratch_shapes=[pltpu.VMEM((k, tn), BF16)],
    )
    return pl.pallas_call(
        functools.partial(_down_group_kernel, n_tiles=n_tiles),
        grid_spec=grid_spec,
        out_shape=jax.ShapeDtypeStruct((rows, d // 2), U32),
        compiler_params=_cparams(2),
        name="down_grouped",
    )(te, a, w_down)


def _route_kernel(lg_ref, tbl_ref, cnt_ref, carry_ref):
    i = pl.program_id(0)

    @pl.when(i == 0)
    def _():
        carry_ref[...] = jnp.zeros_like(carry_ref)

    tm = lg_ref.shape[0]
    lane = lax.broadcasted_iota(I32, (tm, LANES), 1).astype(F32)
    neg = jnp.float32(-jnp.inf)
    lg = jnp.where(lane < N_EXPERTS, lg_ref[...], neg)
    m1 = jnp.max(lg, axis=-1, keepdims=True)
    e1 = jnp.min(jnp.where(lg == m1, lane, float(LANES)), axis=-1, keepdims=True)
    lg2 = jnp.where(lane == e1, neg, lg)
    m2 = jnp.max(lg2, axis=-1, keepdims=True)
    e2 = jnp.min(jnp.where(lg2 == m2, lane, float(LANES)), axis=-1, keepdims=True)
    tt = jnp.exp(m2 - m1)
    w1 = 1.0 / (1.0 + tt)
    w2 = tt * w1

    memb = jnp.logical_or(lane == e1, lane == e2).astype(F32)
    rr = lax.broadcasted_iota(I32, (tm, tm), 0)
    cc = lax.broadcasted_iota(I32, (tm, tm), 1)
    strict_lower = (cc < rr).astype(BF16)
    pos = jnp.dot(strict_lower, memb.astype(BF16), preferred_element_type=F32) + carry_ref[...]
    p1 = jnp.sum(jnp.where(lane == e1, pos, 0.0), axis=-1, keepdims=True)
    p2 = jnp.sum(jnp.where(lane == e2, pos, 0.0), axis=-1, keepdims=True)
    carry_ref[...] = carry_ref[...] + jnp.sum(memb, axis=0, keepdims=True)

    out = jnp.where(lane == 0, e1, 0.0)
    out = jnp.where(lane == 1, e2, out)
    out = jnp.where(lane == 2, p1, out)
    out = jnp.where(lane == 3, p2, out)
    out = jnp.where(lane == 4, w1, out)
    out = jnp.where(lane == 5, w2, out)
    tbl_ref[...] = out
    cnt_ref[...] = carry_ref[...]


def _route(logits):
    t = logits.shape[0]
    tm = min(TM_ROUTE, t)
    return pl.pallas_call(
        _route_kernel,
        grid=(t // tm,),
        in_specs=[pl.BlockSpec((tm, LANES), lambda i: (i, 0))],
        out_specs=[
            pl.BlockSpec((tm, LANES), lambda i: (i, 0)),
            pl.BlockSpec((1, LANES), lambda i: (0, 0)),
        ],
        out_shape=[
            jax.ShapeDtypeStruct((t, LANES), F32),
            jax.ShapeDtypeStruct((1, LANES), F32),
        ],
        scratch_shapes=[pltpu.VMEM((1, LANES), F32)],
        compiler_params=_cparams(1),
        name="moe_route",
    )(logits)


def _dispatch_kernel(dest_ref, plan_ref, hp_ref, xs_ref, zero_ref, sem, zsem, *, tile):
    ch = hp_ref.shape[0]
    n_tiles = xs_ref.shape[0] // tile

    @pl.when(pl.program_id(0) == 0)
    def _():
        zero_ref[...] = jnp.zeros_like(zero_ref)

        def zero_block(start):
            return pltpu.make_async_copy(zero_ref, xs_ref.at[pl.ds(start, tile)], zsem)

        for e in range(N_EXPERTS):
            blk = zero_block(pl.multiple_of(jnp.maximum(plan_ref[e] - tile, 0), tile))
            blk.start()
            blk.wait()

        def tail_start(k, carry):
            zero_block(pl.multiple_of(k * tile, tile)).start()
            return carry

        def tail_wait(k, carry):
            zero_block(0).wait()
            return carry

        lax.fori_loop(plan_ref[N_EXPERTS], n_tiles, tail_start, 0)
        lax.fori_loop(plan_ref[N_EXPERTS], n_tiles, tail_wait, 0)

    def issue(r, carry):
        src = hp_ref.at[pl.ds(r, 1)]
        pltpu.make_async_copy(src, xs_ref.at[pl.ds(dest_ref[2 * r], 1)], sem).start()
        pltpu.make_async_copy(src, xs_ref.at[pl.ds(dest_ref[2 * r + 1], 1)], sem).start()
        return carry

    lax.fori_loop(0, ch, issue, 0, unroll=8)

    for _ in range(2):
        pltpu.make_async_copy(hp_ref, xs_ref.at[pl.ds(0, ch)], sem).wait()


def _dispatch(hp, dest_flat, plan, rows, tile):
    t, w = hp.shape
    ch = min(CH_DISP, t)
    return pl.pallas_call(
        functools.partial(_dispatch_kernel, tile=tile),
        grid=(t // ch,),
        in_specs=[
            pl.BlockSpec((2 * ch,), lambda i: (i,), memory_space=pltpu.SMEM),
            pl.BlockSpec((N_EXPERTS + 1,), lambda i: (0,), memory_space=pltpu.SMEM),
            pl.BlockSpec((ch, w), lambda i: (i, 0)),
        ],
        out_specs=pl.BlockSpec(memory_space=pl.ANY),
        out_shape=jax.ShapeDtypeStruct((rows, w), hp.dtype),
        scratch_shapes=[
            pltpu.VMEM((tile, w), hp.dtype),
            pltpu.SemaphoreType.DMA(()),
            pltpu.SemaphoreType.DMA(()),
        ],
        compiler_params=_cparams(1),
        name="moe_dispatch",
    )(dest_flat, plan, hp)


def _combine_kernel(dcur_ref, dnext_ref, ys_ref, x_ref, tbl_ref, mod_ref, ada_ref, o_ref, buf, sem, *, gate_idx):
    i = pl.program_id(0)
    n = pl.num_programs(0)
    tm = x_ref.shape[0]
    slot = i % 2

    def issue(d_ref, sl):
        def body(r, carry):
            for k in range(2):
                pltpu.make_async_copy(
                    ys_ref.at[pl.ds(d_ref[2 * r + k], 1)], buf.at[sl, k, pl.ds(r, 1)], sem.at[sl]).start()
            return carry

        lax.fori_loop(0, tm, body, 0, unroll=8)

    @pl.when(i == 0)
    def _():
        issue(dcur_ref, 0)

    @pl.when(i + 1 < n)
    def _():
        issue(dnext_ref, 1 - slot)

    for k in range(2):
        pltpu.make_async_copy(ys_ref.at[pl.ds(0, tm)], buf.at[slot, k], sem.at[slot]).wait()

    gate = _gate_row(mod_ref, ada_ref, gate_idx)
    hi_mask = jnp.uint32(0xFFFF0000)
    half = TN // 2
    rb = 32

    def rows(r, carry):
        sl = pl.ds(pl.multiple_of(r * rb, rb), rb)
        w1 = tbl_ref[sl, 4:5]
        w2 = tbl_ref[sl, 5:6]
        for jb in range(o_ref.shape[1] // TN):
            y1 = buf[slot, 0, sl, jb * half:(jb + 1) * half]
            y2 = buf[slot, 1, sl, jb * half:(jb + 1) * half]
            f_lo = w1 * lax.bitcast_convert_type(y1 << 16, F32) + w2 * lax.bitcast_convert_type(y2 << 16, F32)
            f_hi = (w1 * lax.bitcast_convert_type(y1 & hi_mask, F32)
                    + w2 * lax.bitcast_convert_type(y2 & hi_mask, F32))
            for part, f in ((0, f_lo), (1, f_hi)):
                cols = slice(jb * TN + part * half, jb * TN + (part + 1) * half)
                o_ref[sl, cols] = x_ref[sl, cols] + gate[:, cols] * f
        return carry

    lax.fori_loop(0, tm // rb, rows, 0)


def _combine(ys, dest_flat, x2, tbl, mod, ada_table, layer, s_len, gate_idx):
    t, d = x2.shape
    tm = TM_COMB
    per_b = s_len // tm
    n = t // tm
    return pl.pallas_call(
        functools.partial(_combine_kernel, gate_idx=gate_idx),
        grid=(n,),
        in_specs=[
            pl.BlockSpec((2 * tm,), lambda i: (i,), memory_space=pltpu.SMEM),
            pl.BlockSpec((2 * tm,), lambda i: (jnp.minimum(i + 1, n - 1),), memory_space=pltpu.SMEM),
            pl.BlockSpec(memory_space=pl.ANY),
            pl.BlockSpec((tm, d), lambda i: (i, 0)),
            pl.BlockSpec((tm, LANES), lambda i: (i, 0)),
            pl.BlockSpec((1, N_MOD, d), lambda i: (i // per_b, 0, 0)),
            pl.BlockSpec((None, N_MOD, d), lambda i: (layer, 0, 0)),
        ],
        out_specs=pl.BlockSpec((tm, d), lambda i: (i, 0)),
        out_shape=jax.ShapeDtypeStruct((t, d), F32),
        scratch_shapes=[
            pltpu.VMEM((2, 2, tm, d // 2), U32),
            pltpu.SemaphoreType.DMA((2,)),
        ],
        compiler_params=_cparams(1),
        name="moe_combine",
    )(dest_flat, dest_flat, ys, x2, tbl, mod, ada_table)


def _moe_plan(tbl, cnt, tm, n_tiles):
    e = tbl[:, 0:2].astype(I32)
    pos = tbl[:, 2:4].astype(I32)
    counts = cnt[0, :N_EXPERTS].astype(I32)
    padded = ((counts + tm - 1) // tm) * tm
    ends = jnp.cumsum(padded)
    offs = ends - padded
    expert_ids = jnp.arange(N_EXPERTS, dtype=I32)
    dest = jnp.sum(jnp.where(e[..., None] == expert_ids, offs, 0), axis=-1) + pos
    tile_start = jnp.arange(n_tiles, dtype=I32) * tm
    n_used = ends[-1] // tm
    te = jnp.sum((tile_start[:, None] >= ends[None, :]).astype(I32), axis=1)
    last = jnp.minimum(jnp.maximum(n_used - 1, 0), n_tiles - 1)
    te = jnp.where(jnp.arange(n_tiles) < n_used, jnp.minimum(te, N_EXPERTS - 1), te[last])
    te = jnp.concatenate([te.astype(I32), n_used.astype(I32)[None]])
    plan = jnp.concatenate([ends.astype(I32), n_used.astype(I32)[None]])
    return dest.reshape(-1), te, plan


def kernel(x, c, w_ada, b_ada, ada_table, norm_mix_w, w_in, lb_logits, hgrn_norm_w, q_norm_w, k_norm_w,
           rel_bias, w_pool, pool_scale, w_o, norm_ffn_w, ffn_w_gate, ffn_w_up, ffn_w_down, moe_w_router,
           moe_b_router, moe_w_gate, moe_w_up, moe_w_down):
    bsz, s_len, d = x.shape
    depth = w_in.shape[0]
    t = bsz * s_len
    x2 = x.reshape(t, d)

    mod = _adaln_mod(c, w_ada, b_ada).reshape(bsz, N_MOD, d)

    lb_p = jax.nn.softmax(lb_logits.astype(F32), axis=0)
    lb_all = (jnp.cumsum(lb_p, axis=0) - lb_p[0:1]).reshape(depth, A_HEADS, 1, HEAD_DIM)
    bias = _group_bias(rel_bias)
    pool_scale3 = pool_scale.reshape(depth, 1, C_WIDTH)
    n_moe = moe_w_router.shape[0]
    w_router_pad = jnp.pad(moe_w_router, ((0, 0), (0, 0), (0, LANES - N_EXPERTS)))
    b_router_pad = jnp.pad(moe_b_router, ((0, 0), (0, LANES - N_EXPERTS))).reshape(n_moe, 1, LANES)
    tm_dense = min(TM, s_len)
    dense_te = jnp.concatenate([jnp.zeros((t // tm_dense,), I32), jnp.full((1,), t // tm_dense, I32)])

    for l in range(depth):
        h = _norm(x2, mod, ada_table, norm_mix_w, l, s_len, 0)
        proj = _in_proj(h, w_in, l, bsz, s_len)
        y_a, y_b = _hgrn_attention(proj, lb_all[l], hgrn_norm_w[l].reshape(1, HEAD_DIM), bias,
                                   q_norm_w[l].reshape(1, HEAD_DIM), k_norm_w[l].reshape(1, HEAD_DIM))
        y_c = _pool_mixer(proj, w_pool, pool_scale3, l)
        x2 = _out_proj(y_a.reshape(t, A_WIDTH), y_b.reshape(t, B_WIDTH), y_c.reshape(t, C_WIDTH),
                       w_o, x2, mod, ada_table, l, s_len, 2)
        if l % 2 == 0:
            j = l // 2
            h = _norm(x2, mod, ada_table, norm_ffn_w, l, s_len, 3)
            a = _glu(h, ffn_w_gate, ffn_w_up, dense_te, (), tm_dense, TN_GLU, packed=False)
            x2 = _down_residual(a, ffn_w_down, j, 0, 2, x2, mod, ada_table, l, s_len, 5)
            x2 = _down_residual(a, ffn_w_down, j, 1, 2, x2, mod, ada_table, l, s_len, 5)
        else:
            j = l // 2
            hp, logits = _norm_router(x2, mod, ada_table, norm_ffn_w, w_router_pad, b_router_pad, l, j, s_len, 3)
            tbl, cnt = _route(logits)
            rows = 2 * t + N_EXPERTS * TM_E
            n_tiles = rows // TM_E
            dest, te, plan = _moe_plan(tbl, cnt, TM_E, n_tiles)
            xs = _dispatch(hp, dest, plan, rows, TM_E)
            a = _glu(xs, moe_w_gate, moe_w_up, te, (j,), TM_E, TN_GLU_E, packed=True)
            ys = _down_grouped(a, moe_w_down, j, te, TM_E)
            x2 = _combine(ys, dest, x2, tbl, mod, ada_table, l, s_len, 5)
    return x2.reshape(bsz, s_len, d)
```

```python
import functools

import jax
import jax.numpy as jnp
from jax import lax
from jax.experimental import pallas as pl
from jax.experimental.pallas import tpu as pltpu

F32 = jnp.float32
BF16 = jnp.bfloat16
U32 = jnp.uint32
I32 = jnp.int32

D_MODEL = 4096
CHUNK = 64
HEAD_DIM = 128
A_HEADS = 12
B_HEADS = 12
A_WIDTH = A_HEADS * HEAD_DIM
B_WIDTH = B_HEADS * HEAD_DIM
C_WIDTH = D_MODEL - A_WIDTH - B_WIDTH
POOL_WINDOWS = (2, 4, 8, 16)
POOL_GROUP = C_WIDTH // len(POOL_WINDOWS)
IN_COLS = 4 * A_WIDTH + 3 * B_WIDTH + C_WIDTH
IN_HEADS = IN_COLS // HEAD_DIM
BAND_CHUNKS = 9
BAND = BAND_CHUNKS * CHUNK
BAND_PAD = (BAND_CHUNKS - 1) * CHUNK
MAX_REL_DIST = 128
N_EXPERTS = 8
N_MOD = 6
EPS = 1e-6
F_MIN = 1e-6
MASK_VALUE = -1e9

LANES = 128
SUBLANES = 8
VMEM_LIMIT_MB = 56

TM = 1024
TN = 512
TN_OUT = 1024
TN_GLU = 256
TN_GLU_E = 512
TM_E = 512
TM_NORM = 512
TM_ROUTE = 512
TM_COMB = 256
CH_DISP = 256


def _cparams(n_grid, vmem_mb=VMEM_LIMIT_MB):
    return pltpu.CompilerParams(
        dimension_semantics=("arbitrary",) * n_grid,
        vmem_limit_bytes=vmem_mb * 1024 * 1024,
    )


def _silu(x):
    return x * jax.nn.sigmoid(x)


def _cast_weight(w_ref, wbf_ref):
    k = w_ref.shape[0]
    rows = 512 if k % 512 == 0 else 128
    assert k % rows == 0

    def body(r, carry):
        sl = pl.ds(pl.multiple_of(r * rows, rows), rows)
        wbf_ref[sl, :] = w_ref[sl, :].astype(BF16)
        return carry

    lax.fori_loop(0, k // rows, body, 0)


def _mod_kernel(c_ref, w_ref, b_ref, o_ref):
    s = _silu(c_ref[...]).astype(BF16)
    w = w_ref[...].astype(BF16)
    o_ref[...] = jnp.dot(s, w, preferred_element_type=F32) + b_ref[...]


def _adaln_mod(c, w_ada, b_ada):
    bsz, d = c.shape
    n = w_ada.shape[1]
    tn = 1024
    return pl.pallas_call(
        _mod_kernel,
        grid=(n // tn,),
        in_specs=[
            pl.BlockSpec((bsz, d), lambda j: (0, 0)),
            pl.BlockSpec((d, tn), lambda j: (0, j)),
            pl.BlockSpec((1, tn), lambda j: (0, j)),
        ],
        out_specs=pl.BlockSpec((bsz, tn), lambda j: (0, j)),
        out_shape=jax.ShapeDtypeStruct((bsz, n), F32),
        compiler_params=_cparams(1),
        name="adaln_mod",
    )(c, w_ada, b_ada.reshape(1, n))


def _modulated_norm(x_ref, mod_ref, ada_ref, w_ref, shift_idx):
    x = x_ref[...]
    y = x * lax.rsqrt(jnp.mean(x * x, axis=-1, keepdims=True) + EPS) * w_ref[...]
    m = mod_ref[0] + ada_ref[...]
    shift = m[shift_idx:shift_idx + 1]
    scale = m[shift_idx + 1:shift_idx + 2]
    return y * (1.0 + scale) + shift


def _norm_kernel(x_ref, mod_ref, ada_ref, w_ref, h_ref, *, shift_idx):
    h_ref[...] = _modulated_norm(x_ref, mod_ref, ada_ref, w_ref, shift_idx).astype(BF16)


def _pack_bf16_pairs(h):
    n = h.shape[1] // 2
    lo = lax.bitcast_convert_type(h[:, :n].astype(BF16).astype(F32), U32)
    hi = lax.bitcast_convert_type(h[:, n:].astype(BF16).astype(F32), U32)
    return (hi & jnp.uint32(0xFFFF0000)) | (lo >> 16)


def _unpack_bf16_pairs(u):
    lo = lax.bitcast_convert_type(u << 16, F32).astype(BF16)
    hi = lax.bitcast_convert_type(u & jnp.uint32(0xFFFF0000), F32).astype(BF16)
    return lo, hi


def _norm_router_kernel(x_ref, mod_ref, ada_ref, w_ref, wr_ref, br_ref, hp_ref, lg_ref, *, shift_idx):
    h = _modulated_norm(x_ref, mod_ref, ada_ref, w_ref, shift_idx)
    hp_ref[...] = _pack_bf16_pairs(h)
    wr = wr_ref[...]
    h_hi = h.astype(BF16)
    h_lo = (h - h_hi.astype(F32)).astype(BF16)
    w_hi = wr.astype(BF16)
    w_lo = (wr - w_hi.astype(F32)).astype(BF16)
    lg = jnp.dot(h_hi, w_hi, preferred_element_type=F32)
    lg = lg + jnp.dot(h_hi, w_lo, preferred_element_type=F32)
    lg = lg + jnp.dot(h_lo, w_hi, preferred_element_type=F32)
    lg_ref[...] = lg + br_ref[...]


def _norm_specs(t, d, s_len, tm, layer):
    per_b = s_len // tm
    return [
        pl.BlockSpec((tm, d), lambda i: (i, 0)),
        pl.BlockSpec((1, N_MOD, d), lambda i: (i // per_b, 0, 0)),
        pl.BlockSpec((None, N_MOD, d), lambda i: (layer, 0, 0)),
        pl.BlockSpec((None, 1, d), lambda i: (layer, 0, 0)),
    ]


def _norm(x2, mod, ada_table, norm_w, layer, s_len, shift_idx):
    t, d = x2.shape
    tm = min(TM_NORM, s_len)
    return pl.pallas_call(
        functools.partial(_norm_kernel, shift_idx=shift_idx),
        grid=(t // tm,),
        in_specs=_norm_specs(t, d, s_len, tm, layer),
        out_specs=pl.BlockSpec((tm, d), lambda i: (i, 0)),
        out_shape=jax.ShapeDtypeStruct((t, d), BF16),
        compiler_params=_cparams(1),
        name="norm_mod",
    )(x2, mod, ada_table, norm_w.reshape(norm_w.shape[0], 1, d))


def _norm_router(x2, mod, ada_table, norm_w, w_router_pad, b_router_pad, layer, moe_idx, s_len, shift_idx):
    t, d = x2.shape
    tm = min(TM_NORM, s_len)
    return pl.pallas_call(
        functools.partial(_norm_router_kernel, shift_idx=shift_idx),
        grid=(t // tm,),
        in_specs=_norm_specs(t, d, s_len, tm, layer) + [
            pl.BlockSpec((None, d, LANES), lambda i: (moe_idx, 0, 0)),
            pl.BlockSpec((None, 1, LANES), lambda i: (moe_idx, 0, 0)),
        ],
        out_specs=[
            pl.BlockSpec((tm, d // 2), lambda i: (i, 0)),
            pl.BlockSpec((tm, LANES), lambda i: (i, 0)),
        ],
        out_shape=[
            jax.ShapeDtypeStruct((t, d // 2), U32),
            jax.ShapeDtypeStruct((t, LANES), F32),
        ],
        compiler_params=_cparams(1),
        name="norm_mod_router",
    )(x2, mod, ada_table, norm_w.reshape(norm_w.shape[0], 1, d), w_router_pad, b_router_pad)


def _inproj_kernel(h_ref, w_ref, o_ref, wbf_ref):
    @pl.when(pl.program_id(1) == 0)
    def _():
        _cast_weight(w_ref, wbf_ref)

    acc = jnp.dot(h_ref[...], wbf_ref[...], preferred_element_type=F32)
    for hh in range(o_ref.shape[1]):
        o_ref[0, hh] = acc[:, hh * HEAD_DIM:(hh + 1) * HEAD_DIM].astype(BF16)


def _in_proj(h, w_in, layer, bsz, s_len):
    t, d = h.shape
    n = w_in.shape[2]
    tm = min(TM, s_len)
    tn = TN
    per_b = s_len // tm
    hpt = tn // HEAD_DIM
    return pl.pallas_call(
        _inproj_kernel,
        grid=(n // tn, t // tm),
        in_specs=[
            pl.BlockSpec((tm, d), lambda j, i: (i, 0)),
            pl.BlockSpec((None, d, tn), lambda j, i: (layer, 0, j)),
        ],
        out_specs=pl.BlockSpec((1, hpt, tm, HEAD_DIM), lambda j, i: (i // per_b, j, i % per_b, 0)),
        out_shape=jax.ShapeDtypeStruct((bsz, n // HEAD_DIM, s_len, HEAD_DIM), BF16),
        scratch_shapes=[pltpu.VMEM((d, tn), BF16)],
        compiler_params=_cparams(2),
        name="in_proj",
    )(h, w_in)


_HGRN_LEVELS = (32, 16, 8, 4, 2, 1)
_ATTN_GROUP = 4
_ATTN_QROWS = _ATTN_GROUP * CHUNK
_ATTN_KROWS = BAND_PAD + _ATTN_QROWS


def _row_bcast(x, period, row):
    n, c = x.shape
    x3 = x.reshape(n // period, period, c)
    return jnp.broadcast_to(x3[:, row:row + 1, :], x3.shape).reshape(n, c)


def _hgrn_chunk_fn(q_ref, z_ref, v_ref, g_ref, lb_ref, nw_ref, o_ref, st_ref):
    lb = lb_ref[0]
    nw = nw_ref[...]
    row = lax.broadcasted_iota(I32, (CHUNK, HEAD_DIM), 0)
    sub3 = lax.broadcasted_iota(I32, (CHUNK // SUBLANES, SUBLANES, HEAD_DIM), 1)
    r64 = lax.broadcasted_iota(I32, (CHUNK, CHUNK), 0)
    c64 = lax.broadcasted_iota(I32, (CHUNK, CHUNK), 1)
    level_masks = []
    for hs in _HGRN_LEVELS:
        pair = 2 * hs
        same = (r64 & ~(pair - 1)) == (c64 & ~(pair - 1))
        level_masks.append(same & ((r64 & (pair - 1)) >= hs) & ((c64 & (pair - 1)) < hs))

    st_ref[...] = jnp.zeros_like(st_ref)

    def chunk(ci, carry):
        sl = pl.ds(pl.multiple_of(ci * CHUNK, CHUNK), CHUNK)
        q_bf = q_ref[0, 0, sl, :]
        q = q_bf.astype(F32)
        z = z_ref[0, 0, sl, :].astype(F32)
        v_bf = v_ref[0, 0, sl, :]
        g = g_ref[0, 0, sl, :].astype(F32)

        e = jnp.exp(-jnp.abs(z))
        r = 1.0 / (1.0 + e)
        sig_pos = jnp.where(z >= 0, r, e * r)
        sig_neg = jnp.where(z >= 0, e * r, r)
        f_gate = lb + (1.0 - lb) * sig_pos
        log_f = jnp.log2(jnp.maximum(f_gate, F_MIN))
        key = (1.0 - lb) * sig_neg

        grp = CHUNK // SUBLANES
        b3 = log_f.reshape(grp, SUBLANES, HEAD_DIM)
        for sh in (1, 2, 4):
            b3 = b3 + jnp.where(sub3 >= sh, pltpu.roll(b3, sh, axis=1), 0.0)
        run = jnp.zeros((1, 1, HEAD_DIM), F32)
        parts = []
        for gidx in range(grp):
            parts.append(b3[gidx:gidx + 1] + run)
            run = run + b3[gidx:gidx + 1, SUBLANES - 1:SUBLANES, :]
        b = jnp.concatenate(parts, axis=0).reshape(CHUNK, HEAD_DIM)
        b_last = run.reshape(1, HEAD_DIM)

        m4 = row & 3
        key_bf = key.astype(BF16)
        scores = jnp.zeros((CHUNK, CHUNK), F32)
        for hs, mask in zip(_HGRN_LEVELS, level_masks):
            if hs >= 4:
                x = -jnp.abs(b - _row_bcast(b, 2 * hs, hs - 1))
            elif hs == 2:
                nxt = pltpu.roll(log_f, CHUNK - 1, axis=0)
                prv = pltpu.roll(log_f, 1, axis=0)
                x = jnp.where(m4 == 0, nxt, jnp.where(m4 == 1, 0.0, jnp.where(m4 == 2, log_f, log_f + prv)))
            else:
                x = jnp.where((row & 1) == 1, log_f, 0.0)
            w = jnp.exp2(x)
            a_l = (q * w).astype(BF16)
            k_l = (key * w).astype(BF16)
            s_l = lax.dot_general(a_l, k_l, (((1,), (1,)), ((), ())), preferred_element_type=F32)
            scores = scores + jnp.where(mask, s_l, 0.0)
        s_d = lax.dot_general(q_bf, key_bf, (((1,), (1,)), ((), ())), preferred_element_type=F32)
        scores = scores + jnp.where(r64 == c64, s_d, 0.0)
        intra = jnp.dot(scores.astype(BF16), v_bf, preferred_element_type=F32)

        st = st_ref[...]
        qe = (q * jnp.exp2(b)).astype(BF16)
        inter = lax.dot_general(qe, st.astype(BF16), (((1,), (1,)), ((), ())), preferred_element_type=F32)
        kd = (key * jnp.exp2(b_last - b)).astype(BF16)
        upd = lax.dot_general(v_bf, kd, (((0,), (0,)), ((), ())), preferred_element_type=F32)
        st_ref[...] = st * jnp.exp2(b_last) + upd

        out = intra + inter
        y = out * lax.rsqrt(jnp.mean(out * out, axis=-1, keepdims=True) + EPS) * nw
        o_ref[0, sl, :] = (y * _silu(g)).astype(BF16)
        return carry

    return chunk


def _head_rms(x, w):
    return x * lax.rsqrt(jnp.mean(x * x, axis=-1, keepdims=True) + EPS) * w


def _attn_group_fn(q_ref, k_ref, v_ref, bias_ref, qw_ref, kw_ref, o_ref, kp_ref, vp_ref):
    s_len = q_ref.shape[2]
    qw = qw_ref[...]
    kw = kw_ref[...]
    scale = HEAD_DIM ** -0.5
    col = lax.broadcasted_iota(I32, (_ATTN_QROWS, _ATTN_KROWS), 1)

    kp_ref[0:BAND_PAD, :] = jnp.zeros((BAND_PAD, HEAD_DIM), BF16)
    vp_ref[0:BAND_PAD, :] = jnp.zeros((BAND_PAD, HEAD_DIM), BF16)
    blk = _ATTN_QROWS

    def prep(i, carry):
        src = pl.ds(pl.multiple_of(i * blk, blk), blk)
        dst = pl.ds(pl.multiple_of(BAND_PAD + i * blk, CHUNK), blk)
        kp_ref[dst, :] = _head_rms(k_ref[0, 0, src, :].astype(F32), kw).astype(BF16)
        vp_ref[dst, :] = v_ref[0, 0, src, :]
        return carry

    lax.fori_loop(0, s_len // blk, prep, 0, unroll=2 if (s_len // blk) % 2 == 0 else 1)

    def group(gi, carry):
        r0 = pl.multiple_of(gi * _ATTN_QROWS, _ATTN_QROWS)
        qn = _head_rms(q_ref[0, 0, pl.ds(r0, _ATTN_QROWS), :].astype(F32), qw).astype(BF16)
        kc = kp_ref[pl.ds(r0, _ATTN_KROWS), :]
        vc = vp_ref[pl.ds(r0, _ATTN_KROWS), :]
        s = lax.dot_general(qn, kc, (((1,), (1,)), ((), ())), preferred_element_type=F32) * scale + bias_ref[0]
        s = jnp.where(col + r0 >= BAND_PAD, s, MASK_VALUE)
        m = jnp.max(s, axis=-1, keepdims=True)
        p = jnp.exp(s - m)
        den = jnp.sum(p, axis=-1, keepdims=True)
        o = jnp.dot(p.astype(BF16), vc, preferred_element_type=F32) / den
        o_ref[0, pl.ds(r0, _ATTN_QROWS), :] = o.astype(BF16)
        return carry

    return group


def _mixers_kernel(qa_ref, za_ref, va_ref, ga_ref, lb_ref, nw_ref, qb_ref, kb_ref, vb_ref, bias_ref, qw_ref, kw_ref,
                   oa_ref, ob_ref, st_ref, kp_ref, vp_ref):
    s_len = qa_ref.shape[2]
    chunk = _hgrn_chunk_fn(qa_ref, za_ref, va_ref, ga_ref, lb_ref, nw_ref, oa_ref, st_ref)
    group = _attn_group_fn(qb_ref, kb_ref, vb_ref, bias_ref, qw_ref, kw_ref, ob_ref, kp_ref, vp_ref)

    def body(gi, carry):
        for u in range(_ATTN_GROUP):
            chunk(gi * _ATTN_GROUP + u, carry)
        group(gi, carry)
        return carry

    lax.fori_loop(0, s_len // _ATTN_QROWS, body, 0, unroll=2)


def _hgrn_attention(proj_hm, lb_l, norm_w_l, bias_grp, qn_w_l, kn_w_l):
    bsz, _, s_len, _ = proj_hm.shape
    assert s_len % _ATTN_QROWS == 0 and A_HEADS == B_HEADS

    def slab(off):
        return pl.BlockSpec((1, 1, s_len, HEAD_DIM), lambda h, b: (b, off + h, 0, 0))

    vec = pl.BlockSpec((1, HEAD_DIM), lambda h, b: (0, 0))
    out = pl.BlockSpec((1, s_len, HEAD_DIM), lambda h, b: (b, 0, h))
    base = 4 * A_HEADS
    return pl.pallas_call(
        _mixers_kernel,
        grid=(A_HEADS, bsz),
        in_specs=[
            slab(0), slab(A_HEADS), slab(2 * A_HEADS), slab(3 * A_HEADS),
            pl.BlockSpec((1, 1, HEAD_DIM), lambda h, b: (h, 0, 0)),
            vec,
            slab(base), slab(base + B_HEADS), slab(base + 2 * B_HEADS),
            pl.BlockSpec((1, _ATTN_QROWS, _ATTN_KROWS), lambda h, b: (h, 0, 0)),
            vec, vec,
        ],
        out_specs=[out, out],
        out_shape=[
            jax.ShapeDtypeStruct((bsz, s_len, A_WIDTH), BF16),
            jax.ShapeDtypeStruct((bsz, s_len, B_WIDTH), BF16),
        ],
        scratch_shapes=[
            pltpu.VMEM((HEAD_DIM, HEAD_DIM), F32),
            pltpu.VMEM((BAND_PAD + s_len, HEAD_DIM), BF16),
            pltpu.VMEM((BAND_PAD + s_len, HEAD_DIM), BF16),
        ],
        compiler_params=_cparams(2),
        name="hgrn_attention",
    )(proj_hm, proj_hm, proj_hm, proj_hm, lb_l, norm_w_l, proj_hm, proj_hm, proj_hm, bias_grp, qn_w_l, kn_w_l)


def _group_bias(rel_bias):
    diag = jnp.arange(-(CHUNK - 1), BAND)
    idx = jnp.clip(BAND_PAD - diag, -MAX_REL_DIST, MAX_REL_DIST) + MAX_REL_DIST
    per_diag = jnp.take(rel_bias.astype(F32), idx, axis=1)
    bias = jnp.stack([per_diag[:, CHUNK - 1 - c:CHUNK - 1 - c + BAND] for c in range(CHUNK)], axis=1)
    blocks = [
        jnp.pad(bias, ((0, 0), (0, 0), (a * CHUNK, _ATTN_KROWS - BAND - a * CHUNK)), constant_values=MASK_VALUE)
        for a in range(_ATTN_GROUP)
    ]
    return jnp.concatenate(blocks, axis=1)


_POOL_HALO = 16


def _pool_kernel(p_ref, w_ref, sc_ref, o_ref, pad_ref, wbf_ref):
    s_len = p_ref.shape[2]
    gi = pl.program_id(1)
    blk = 256 if s_len % 256 == 0 else CHUNK
    nblk = s_len // blk
    hp = p_ref.shape[1]
    pad_ref[0:_POOL_HALO, :] = jnp.zeros((_POOL_HALO, pad_ref.shape[1]), F32)
    for hh in range(hp):
        pad_ref[_POOL_HALO:_POOL_HALO + s_len, hh * HEAD_DIM:(hh + 1) * HEAD_DIM] = p_ref[0, hh].astype(F32)
    wbf_ref[...] = w_ref[0].astype(BF16)
    scale = sc_ref[...]

    for g, win in enumerate(POOL_WINDOWS):
        @pl.when(gi == g)
        def _(win=win):
            def body(i, carry):
                r0 = pl.multiple_of(i * blk, blk)
                acc = pad_ref[pl.ds(r0, _POOL_HALO + blk), :]
                cur = acc[_POOL_HALO:, :]
                span = 1
                while span < win:
                    acc = acc + pltpu.roll(acc, span, axis=0)
                    span *= 2
                acc = acc[_POOL_HALO:, :]
                pos = lax.broadcasted_iota(I32, acc.shape, 0) + r0 + 1
                cnt = jnp.minimum(pos, win).astype(F32)
                mixed = (acc / cnt - cur).astype(BF16)
                out = jnp.dot(mixed, wbf_ref[...], preferred_element_type=F32) * scale
                o_ref[0, pl.ds(r0, blk), :] = out.astype(BF16)
                return carry

            lax.fori_loop(0, nblk, body, 0)


def _pool_mixer(proj_hm, w_pool, pool_scale3, layer):
    bsz, _, s_len, _ = proj_hm.shape
    hp = POOL_GROUP // HEAD_DIM
    base = (4 * A_HEADS + 3 * B_HEADS) // hp
    n_grp = len(POOL_WINDOWS)
    return pl.pallas_call(
        _pool_kernel,
        grid=(bsz, n_grp),
        in_specs=[
            pl.BlockSpec((1, hp, s_len, HEAD_DIM), lambda b, g: (b, base + g, 0, 0)),
            pl.BlockSpec((None, 1, POOL_GROUP, POOL_GROUP), lambda b, g: (layer, g, 0, 0)),
            pl.BlockSpec((None, 1, POOL_GROUP), lambda b, g: (layer, 0, g)),
        ],
        out_specs=pl.BlockSpec((1, s_len, POOL_GROUP), lambda b, g: (b, 0, g)),
        out_shape=jax.ShapeDtypeStruct((bsz, s_len, C_WIDTH), BF16),
        scratch_shapes=[
            pltpu.VMEM((_POOL_HALO + s_len, POOL_GROUP), F32),
            pltpu.VMEM((POOL_GROUP, POOL_GROUP), BF16),
        ],
        compiler_params=_cparams(2),
        name="pool_mixer",
    )(proj_hm, w_pool, pool_scale3)


def _gate_row(mod_ref, ada_ref, gate_idx):
    return mod_ref[0, gate_idx:gate_idx + 1, :] + ada_ref[gate_idx:gate_idx + 1, :]


def _outproj_kernel(ya_ref, yb_ref, yc_ref, w_ref, x_ref, mod_ref, ada_ref, o_ref, wbf_ref, *, gate_idx):
    @pl.when(pl.program_id(1) == 0)
    def _():
        _cast_weight(w_ref, wbf_ref)

    acc = jnp.dot(ya_ref[...], wbf_ref[0:A_WIDTH, :], preferred_element_type=F32)
    acc = acc + jnp.dot(yb_ref[...], wbf_ref[A_WIDTH:A_WIDTH + B_WIDTH, :], preferred_element_type=F32)
    acc = acc + jnp.dot(yc_ref[...], wbf_ref[A_WIDTH + B_WIDTH:, :], preferred_element_type=F32)
    o_ref[...] = x_ref[...] + _gate_row(mod_ref, ada_ref, gate_idx) * acc


def _out_proj(ya, yb, yc, w_o, x2, mod, ada_table, layer, s_len, gate_idx):
    t, d = x2.shape
    tm = min(TM_E, s_len)
    tn = TN_OUT
    per_b = s_len // tm
    return pl.pallas_call(
        functools.partial(_outproj_kernel, gate_idx=gate_idx),
        grid=(d // tn, t // tm),
        in_specs=[
            pl.BlockSpec((tm, A_WIDTH), lambda j, i: (i, 0)),
            pl.BlockSpec((tm, B_WIDTH), lambda j, i: (i, 0)),
            pl.BlockSpec((tm, C_WIDTH), lambda j, i: (i, 0)),
            pl.BlockSpec((None, d, tn), lambda j, i: (layer, 0, j), pipeline_mode=pl.Buffered(1)),
            pl.BlockSpec((tm, tn), lambda j, i: (i, j)),
            pl.BlockSpec((1, N_MOD, tn), lambda j, i: (i // per_b, 0, j)),
            pl.BlockSpec((None, N_MOD, tn), lambda j, i: (layer, 0, j)),
        ],
        out_specs=pl.BlockSpec((tm, tn), lambda j, i: (i, j)),
        out_shape=jax.ShapeDtypeStruct((t, d), F32),
        scratch_shapes=[pltpu.VMEM((d, tn), BF16)],
        compiler_params=_cparams(2),
        name="out_proj",
    )(ya, yb, yc, w_o, x2, mod, ada_table)


def _tile_flags(te_ref, n_tiles):
    i = pl.program_id(1)
    prev = te_ref[jnp.maximum(i - 1, 0)]
    first = jnp.logical_or(i == 0, te_ref[i] != prev)
    valid = i < te_ref[n_tiles]
    return first, valid


def _glu_kernel(te_ref, x_ref, wg_ref, wu_ref, o_ref, wg_bf, wu_bf, *, n_tiles, packed):
    first, valid = _tile_flags(te_ref, n_tiles)

    @pl.when(first)
    def _():
        _cast_weight(wg_ref, wg_bf)
        _cast_weight(wu_ref, wu_bf)

    @pl.when(valid)
    def _():
        if packed:
            half = x_ref.shape[1]
            lo, hi = _unpack_bf16_pairs(x_ref[...])
            g = jnp.dot(lo, wg_bf[0:half, :], preferred_element_type=F32)
            g = g + jnp.dot(hi, wg_bf[half:, :], preferred_element_type=F32)
            u = jnp.dot(lo, wu_bf[0:half, :], preferred_element_type=F32)
            u = u + jnp.dot(hi, wu_bf[half:, :], preferred_element_type=F32)
        else:
            x = x_ref[...]
            g = jnp.dot(x, wg_bf[...], preferred_element_type=F32)
            u = jnp.dot(x, wu_bf[...], preferred_element_type=F32)
        o_ref[...] = (_silu(g) * u).astype(BF16)

    @pl.when(jnp.logical_not(valid))
    def _():
        o_ref[...] = jnp.zeros_like(o_ref)


def _glu(x, wg, wu, te, widx, tm, tn, packed):
    rows, xc = x.shape
    d, f = wg.shape[-2], wg.shape[-1]
    n_tiles = rows // tm
    lead = len(wg.shape) - 3

    def w_map(j, i, te_ref):
        return tuple(widx[:lead]) + (te_ref[i], 0, j)

    w_block = (None,) * (lead + 1) + (d, tn)
    grid_spec = pltpu.PrefetchScalarGridSpec(
        num_scalar_prefetch=1,
        grid=(f // tn, n_tiles),
        in_specs=[
            pl.BlockSpec((tm, xc), lambda j, i, te_ref: (i, 0)),
            pl.BlockSpec(w_block, w_map),
            pl.BlockSpec(w_block, w_map),
        ],
        out_specs=pl.BlockSpec((tm, tn), lambda j, i, te_ref: (i, j)),
        scratch_shapes=[pltpu.VMEM((d, tn), BF16), pltpu.VMEM((d, tn), BF16)],
    )
    return pl.pallas_call(
        functools.partial(_glu_kernel, n_tiles=n_tiles, packed=packed),
        grid_spec=grid_spec,
        out_shape=jax.ShapeDtypeStruct((rows, f), BF16),
        compiler_params=_cparams(2),
        name="glu_pair",
    )(te, x, wg, wu)


def _down_res_kernel(a_ref, w_ref, r_ref, mod_ref, ada_ref, o_ref, wbf_ref, *, gate_idx):
    @pl.when(pl.program_id(1) == 0)
    def _():
        _cast_weight(w_ref, wbf_ref)

    acc = jnp.dot(a_ref[...], wbf_ref[...], preferred_element_type=F32)
    o_ref[...] = r_ref[...] + _gate_row(mod_ref, ada_ref, gate_idx) * acc


def _down_residual(a, w_down, widx, k_half, n_half, res, mod, ada_table, layer, s_len, gate_idx):
    t, d = res.shape
    kh = a.shape[1] // n_half
    tm = min(TM, s_len)
    tn = TN
    per_b = s_len // tm
    return pl.pallas_call(
        functools.partial(_down_res_kernel, gate_idx=gate_idx),
        grid=(d // tn, t // tm),
        in_specs=[
            pl.BlockSpec((tm, kh), lambda j, i: (i, k_half)),
            pl.BlockSpec((None, kh, tn), lambda j, i: (widx, k_half, j), pipeline_mode=pl.Buffered(1)),
            pl.BlockSpec((tm, tn), lambda j, i: (i, j)),
            pl.BlockSpec((1, N_MOD, tn), lambda j, i: (i // per_b, 0, j)),
            pl.BlockSpec((None, N_MOD, tn), lambda j, i: (layer, 0, j)),
        ],
        out_specs=pl.BlockSpec((tm, tn), lambda j, i: (i, j)),
        out_shape=jax.ShapeDtypeStruct((t, d), F32),
        scratch_shapes=[pltpu.VMEM((kh, tn), BF16)],
        compiler_params=_cparams(2),
        name="down_residual",
    )(a, w_down, res, mod, ada_table)


def _down_group_kernel(te_ref, a_ref, w_ref, o_ref, wbf_ref, *, n_tiles):
    first, valid = _tile_flags(te_ref, n_tiles)

    @pl.when(first)
    def _():
        _cast_weight(w_ref, wbf_ref)

    @pl.when(valid)
    def _():
        o_ref[...] = _pack_bf16_pairs(jnp.dot(a_ref[...], wbf_ref[...], preferred_element_type=F32))

    @pl.when(jnp.logical_not(valid))
    def _():
        o_ref[...] = jnp.zeros_like(o_ref)


def _down_grouped(a, w_down, moe_idx, te, tm):
    rows, k = a.shape
    d = w_down.shape[-1]
    tn = TN
    n_tiles = rows // tm
    grid_spec = pltpu.PrefetchScalarGridSpec(
        num_scalar_prefetch=1,
        grid=(d // tn, n_tiles),
        in_specs=[
            pl.BlockSpec((tm, k), lambda j, i, te_ref: (i, 0)),
            pl.BlockSpec((None, None, k, tn), lambda j, i, te_ref: (moe_idx, te_ref[i], 0, j)),
        ],
        out_specs=pl.BlockSpec((tm, tn // 2), lambda j, i, te_ref: (i, j)),
        scratch_shapes=[pltpu.VMEM((k, tn), BF16)],
    )
    return pl.pallas_call(
        functools.partial(_down_group_kernel, n_tiles=n_tiles),
        grid_spec=grid_spec,
        out_shape=jax.ShapeDtypeStruct((rows, d // 2), U32),
        compiler_params=_cparams(2),
        name="down_grouped",
    )(te, a, w_down)


def _route_kernel(lg_ref, tbl_ref, cnt_ref, carry_ref):
    i = pl.program_id(0)

    @pl.when(i == 0)
    def _():
        carry_ref[...] = jnp.zeros_like(carry_ref)

    tm = lg_ref.shape[0]
    lane = lax.broadcasted_iota(I32, (tm, LANES), 1).astype(F32)
    neg = jnp.float32(-jnp.inf)
    lg = jnp.where(lane < N_EXPERTS, lg_ref[...], neg)
    m1 = jnp.max(lg, axis=-1, keepdims=True)
    e1 = jnp.min(jnp.where(lg == m1, lane, float(LANES)), axis=-1, keepdims=True)
    lg2 = jnp.where(lane == e1, neg, lg)
    m2 = jnp.max(lg2, axis=-1, keepdims=True)
    e2 = jnp.min(jnp.where(lg2 == m2, lane, float(LANES)), axis=-1, keepdims=True)
    tt = jnp.exp(m2 - m1)
    w1 = 1.0 / (1.0 + tt)
    w2 = tt * w1

    memb = jnp.logical_or(lane == e1, lane == e2).astype(F32)
    rr = lax.broadcasted_iota(I32, (tm, tm), 0)
    cc = lax.broadcasted_iota(I32, (tm, tm), 1)
    strict_lower = (cc < rr).astype(BF16)
    pos = jnp.dot(strict_lower, memb.astype(BF16), preferred_element_type=F32) + carry_ref[...]
    p1 = jnp.sum(jnp.where(lane == e1, pos, 0.0), axis=-1, keepdims=True)
    p2 = jnp.sum(jnp.where(lane == e2, pos, 0.0), axis=-1, keepdims=True)
    carry_ref[...] = carry_ref[...] + jnp.sum(memb, axis=0, keepdims=True)

    out = jnp.where(lane == 0, e1, 0.0)
    out = jnp.where(lane == 1, e2, out)
    out = jnp.where(lane == 2, p1, out)
    out = jnp.where(lane == 3, p2, out)
    out = jnp.where(lane == 4, w1, out)
    out = jnp.where(lane == 5, w2, out)
    tbl_ref[...] = out
    cnt_ref[...] = carry_ref[...]


def _route(logits):
    t = logits.shape[0]
    tm = min(TM_ROUTE, t)
    return pl.pallas_call(
        _route_kernel,
        grid=(t // tm,),
        in_specs=[pl.BlockSpec((tm, LANES), lambda i: (i, 0))],
        out_specs=[
            pl.BlockSpec((tm, LANES), lambda i: (i, 0)),
            pl.BlockSpec((1, LANES), lambda i: (0, 0)),
        ],
        out_shape=[
            jax.ShapeDtypeStruct((t, LANES), F32),
            jax.ShapeDtypeStruct((1, LANES), F32),
        ],
        scratch_shapes=[pltpu.VMEM((1, LANES), F32)],
        compiler_params=_cparams(1),
        name="moe_route",
    )(logits)


def _dispatch_kernel(dest_ref, plan_ref, hp_ref, xs_ref, zero_ref, sem, zsem, *, tile):
    ch = hp_ref.shape[0]
    n_tiles = xs_ref.shape[0] // tile

    @pl.when(pl.program_id(0) == 0)
    def _():
        zero_ref[...] = jnp.zeros_like(zero_ref)

        def zero_block(start):
            return pltpu.make_async_copy(zero_ref, xs_ref.at[pl.ds(start, tile)], zsem)

        for e in range(N_EXPERTS):
            blk = zero_block(pl.multiple_of(jnp.maximum(plan_ref[e] - tile, 0), tile))
            blk.start()
            blk.wait()

        def tail_start(k, carry):
            zero_block(pl.multiple_of(k * tile, tile)).start()
            return carry

        def tail_wait(k, carry):
            zero_block(0).wait()
            return carry

        lax.fori_loop(plan_ref[N_EXPERTS], n_tiles, tail_start, 0)
        lax.fori_loop(plan_ref[N_EXPERTS], n_tiles, tail_wait, 0)

    def issue(r, carry):
        src = hp_ref.at[pl.ds(r, 1)]
        pltpu.make_async_copy(src, xs_ref.at[pl.ds(dest_ref[2 * r], 1)], sem).start()
        pltpu.make_async_copy(src, xs_ref.at[pl.ds(dest_ref[2 * r + 1], 1)], sem).start()
        return carry

    lax.fori_loop(0, ch, issue, 0, unroll=8)

    for _ in range(2):
        pltpu.make_async_copy(hp_ref, xs_ref.at[pl.ds(0, ch)], sem).wait()


def _dispatch(hp, dest_flat, plan, rows, tile):
    t, w = hp.shape
    ch = min(CH_DISP, t)
    return pl.pallas_call(
        functools.partial(_dispatch_kernel, tile=tile),
        grid=(t // ch,),
        in_specs=[
            pl.BlockSpec((2 * ch,), lambda i: (i,), memory_space=pltpu.SMEM),
            pl.BlockSpec((N_EXPERTS + 1,), lambda i: (0,), memory_space=pltpu.SMEM),
            pl.BlockSpec((ch, w), lambda i: (i, 0)),
        ],
        out_specs=pl.BlockSpec(memory_space=pl.ANY),
        out_shape=jax.ShapeDtypeStruct((rows, w), hp.dtype),
        scratch_shapes=[
            pltpu.VMEM((tile, w), hp.dtype),
            pltpu.SemaphoreType.DMA(()),
            pltpu.SemaphoreType.DMA(()),
        ],
        compiler_params=_cparams(1),
        name="moe_dispatch",
    )(dest_flat, plan, hp)


def _combine_kernel(dcur_ref, dnext_ref, ys_ref, x_ref, tbl_ref, mod_ref, ada_ref, o_ref, buf, sem, *, gate_idx):
    i = pl.program_id(0)
    n = pl.num_programs(0)
    tm = x_ref.shape[0]
    slot = i % 2

    def issue(d_ref, sl):
        def body(r, carry):
            for k in range(2):
                pltpu.make_async_copy(
                    ys_ref.at[pl.ds(d_ref[2 * r + k], 1)], buf.at[sl, k, pl.ds(r, 1)], sem.at[sl]).start()
            return carry

        lax.fori_loop(0, tm, body, 0, unroll=8)

    @pl.when(i == 0)
    def _():
        issue(dcur_ref, 0)

    @pl.when(i + 1 < n)
    def _():
        issue(dnext_ref, 1 - slot)

    for k in range(2):
        pltpu.make_async_copy(ys_ref.at[pl.ds(0, tm)], buf.at[slot, k], sem.at[slot]).wait()

    gate = _gate_row(mod_ref, ada_ref, gate_idx)
    hi_mask = jnp.uint32(0xFFFF0000)
    half = TN // 2
    rb = 32

    def rows(r, carry):
        sl = pl.ds(pl.multiple_of(r * rb, rb), rb)
        w1 = tbl_ref[sl, 4:5]
        w2 = tbl_ref[sl, 5:6]
        for jb in range(o_ref.shape[1] // TN):
            y1 = buf[slot, 0, sl, jb * half:(jb + 1) * half]
            y2 = buf[slot, 1, sl, jb * half:(jb + 1) * half]
            f_lo = w1 * lax.bitcast_convert_type(y1 << 16, F32) + w2 * lax.bitcast_convert_type(y2 << 16, F32)
            f_hi = (w1 * lax.bitcast_convert_type(y1 & hi_mask, F32)
                    + w2 * lax.bitcast_convert_type(y2 & hi_mask, F32))
            for part, f in ((0, f_lo), (1, f_hi)):
                cols = slice(jb * TN + part * half, jb * TN + (part + 1) * half)
                o_ref[sl, cols] = x_ref[sl, cols] + gate[:, cols] * f
        return carry

    lax.fori_loop(0, tm // rb, rows, 0)


def _combine(ys, dest_flat, x2, tbl, mod, ada_table, layer, s_len, gate_idx):
    t, d = x2.shape
    tm = TM_COMB
    per_b = s_len // tm
    n = t // tm
    return pl.pallas_call(
        functools.partial(_combine_kernel, gate_idx=gate_idx),
        grid=(n,),
        in_specs=[
            pl.BlockSpec((2 * tm,), lambda i: (i,), memory_space=pltpu.SMEM),
            pl.BlockSpec((2 * tm,), lambda i: (jnp.minimum(i + 1, n - 1),), memory_space=pltpu.SMEM),
            pl.BlockSpec(memory_space=pl.ANY),
            pl.BlockSpec((tm, d), lambda i: (i, 0)),
            pl.BlockSpec((tm, LANES), lambda i: (i, 0)),
            pl.BlockSpec((1, N_MOD, d), lambda i: (i // per_b, 0, 0)),
            pl.BlockSpec((None, N_MOD, d), lambda i: (layer, 0, 0)),
        ],
        out_specs=pl.BlockSpec((tm, d), lambda i: (i, 0)),
        out_shape=jax.ShapeDtypeStruct((t, d), F32),
        scratch_shapes=[
            pltpu.VMEM((2, 2, tm, d // 2), U32),
            pltpu.SemaphoreType.DMA((2,)),
        ],
        compiler_params=_cparams(1),
        name="moe_combine",
    )(dest_flat, dest_flat, ys, x2, tbl, mod, ada_table)


def _moe_plan(tbl, cnt, tm, n_tiles):
    e = tbl[:, 0:2].astype(I32)
    pos = tbl[:, 2:4].astype(I32)
    counts = cnt[0, :N_EXPERTS].astype(I32)
    padded = ((counts + tm - 1) // tm) * tm
    ends = jnp.cumsum(padded)
    offs = ends - padded
    expert_ids = jnp.arange(N_EXPERTS, dtype=I32)
    dest = jnp.sum(jnp.where(e[..., None] == expert_ids, offs, 0), axis=-1) + pos
    tile_start = jnp.arange(n_tiles, dtype=I32) * tm
    n_used = ends[-1] // tm
    te = jnp.sum((tile_start[:, None] >= ends[None, :]).astype(I32), axis=1)
    last = jnp.minimum(jnp.maximum(n_used - 1, 0), n_tiles - 1)
    te = jnp.where(jnp.arange(n_tiles) < n_used, jnp.minimum(te, N_EXPERTS - 1), te[last])
    te = jnp.concatenate([te.astype(I32), n_used.astype(I32)[None]])
    plan = jnp.concatenate([ends.astype(I32), n_used.astype(I32)[None]])
    return dest.reshape(-1), te, plan


def kernel(x, c, w_ada, b_ada, ada_table, norm_mix_w, w_in, lb_logits, hgrn_norm_w, q_norm_w, k_norm_w,
           rel_bias, w_pool, pool_scale, w_o, norm_ffn_w, ffn_w_gate, ffn_w_up, ffn_w_down, moe_w_router,
           moe_b_router, moe_w_gate, moe_w_up, moe_w_down):
    bsz, s_len, d = x.shape
    depth = w_in.shape[0]
    t = bsz * s_len
    x2 = x.reshape(t, d)

    mod = _adaln_mod(c, w_ada, b_ada).reshape(bsz, N_MOD, d)

    lb_p = jax.nn.softmax(lb_logits.astype(F32), axis=0)
    lb_all = (jnp.cumsum(lb_p, axis=0) - lb_p[0:1]).reshape(depth, A_HEADS, 1, HEAD_DIM)
    bias = _group_bias(rel_bias)
    pool_scale3 = pool_scale.reshape(depth, 1, C_WIDTH)
    n_moe = moe_w_router.shape[0]
    w_router_pad = jnp.pad(moe_w_router, ((0, 0), (0, 0), (0, LANES - N_EXPERTS)))
    b_router_pad = jnp.pad(moe_b_router, ((0, 0), (0, LANES - N_EXPERTS))).reshape(n_moe, 1, LANES)
    tm_dense = min(TM, s_len)
    dense_te = jnp.concatenate([jnp.zeros((t // tm_dense,), I32), jnp.full((1,), t // tm_dense, I32)])

    for l in range(depth):
        h = _norm(x2, mod, ada_table, norm_mix_w, l, s_len, 0)
        proj = _in_proj(h, w_in, l, bsz, s_len)
        y_a, y_b = _hgrn_attention(proj, lb_all[l], hgrn_norm_w[l].reshape(1, HEAD_DIM), bias,
                                   q_norm_w[l].reshape(1, HEAD_DIM), k_norm_w[l].reshape(1, HEAD_DIM))
        y_c = _pool_mixer(proj, w_pool, pool_scale3, l)
        x2 = _out_proj(y_a.reshape(t, A_WIDTH), y_b.reshape(t, B_WIDTH), y_c.reshape(t, C_WIDTH),
                       w_o, x2, mod, ada_table, l, s_len, 2)
        if l % 2 == 0:
            j = l // 2
            h = _norm(x2, mod, ada_table, norm_ffn_w, l, s_len, 3)
            a = _glu(h, ffn_w_gate, ffn_w_up, dense_te, (), tm_dense, TN_GLU, packed=False)
            x2 = _down_residual(a, ffn_w_down, j, 0, 2, x2, mod, ada_table, l, s_len, 5)
            x2 = _down_residual(a, ffn_w_down, j, 1, 2, x2, mod, ada_table, l, s_len, 5)
        else:
            j = l // 2
            hp, logits = _norm_router(x2, mod, ada_table, norm_ffn_w, w_router_pad, b_router_pad, l, j, s_len, 3)
            tbl, cnt = _route(logits)
            rows = 2 * t + N_EXPERTS * TM_E
            n_tiles = rows // TM_E
            dest, te, plan = _moe_plan(tbl, cnt, TM_E, n_tiles)
            xs = _dispatch(hp, dest, plan, rows, TM_E)
            a = _glu(xs, moe_w_gate, moe_w_up, te, (j,), TM_E, TN_GLU_E, packed=True)
            ys = _down_grouped(a, moe_w_down, j, te, TM_E)
            x2 = _combine(ys, dest, x2, tbl, mod, ada_table, l, s_len, 5)
    return x2.reshape(bsz, s_len, d)
```
